```python
import jax
import jax.numpy as jnp
from jax import lax
import numpy as np

D_MODEL = 2048
BATCH = 16
SEQ = 2048
DEPTH = 1

HGRN_HEAD_DIM = 128
HGRN_HEADS = D_MODEL // HGRN_HEAD_DIM
HGRN_WIDTH = HGRN_HEADS * HGRN_HEAD_DIM
HGRN_CHUNK = 32
ATTN_GROUPS = ((128, 1), (512, 4), (2048, 16))
ATTN_HEADS_PER_GROUP = 4
HEAD_DIM = 128
ATTN_QKV_WIDTH = len(ATTN_GROUPS) * 3 * ATTN_HEADS_PER_GROUP * HEAD_DIM
ATTN_OUT_WIDTH = ATTN_HEADS_PER_GROUP * HEAD_DIM
ROPE_THETA = 500000.0
ROPE_DIM = HEAD_DIM // 4
N_BRANCHES = 2
IN_COLS = 5 * HGRN_WIDTH + ATTN_QKV_WIDTH + N_BRANCHES * D_MODEL
D_FF = ((8 * D_MODEL // 3 + 255) // 256) * 256
DEEPNORM_ALPHA = (2.0 * DEPTH) ** 0.25
DEEPNORM_BETA = (8.0 * DEPTH) ** -0.25
LN_EPS = 1e-5
NEG_INF = -1e30

kernel_name = 'hybrid_hgrn2_dilated_attn_macaron_deepnorm'


def layer_norm(x, g, b):
    xf = x.astype(jnp.float32)
    mu = jnp.mean(xf, axis=-1, keepdims=True)
    var = jnp.mean(jnp.square(xf - mu), axis=-1, keepdims=True)
    return ((xf - mu) * lax.rsqrt(var + LN_EPS) * g + b).astype(x.dtype)


def swiglu(x, w_in, w_out):
    gate, up = jnp.split(x @ w_in, 2, axis=-1)
    return (jax.nn.silu(gate) * up) @ w_out


def partial_rope(t, pos):
    t = t.astype(jnp.float32)
    inv_freq = ROPE_THETA ** (-jnp.arange(0, ROPE_DIM, 2, dtype=jnp.float32) / ROPE_DIM)
    ang = pos.astype(jnp.float32)[:, None] * inv_freq
    cos = jnp.cos(ang)[None, :, None, None, :]
    sin = jnp.sin(ang)[None, :, None, None, :]
    t1, t2, rest = jnp.split(t, [ROPE_DIM // 2, ROPE_DIM], axis=-1)
    return jnp.concatenate([t1 * cos - t2 * sin, t2 * cos + t1 * sin, rest], axis=-1)


def dilated_window_attention(q, k, v, window, dilation):
    b_, s_, h_, dh = q.shape
    half = window // (2 * dilation)
    seg = s_ // dilation
    blk = half
    n_blk = -(-seg // blk)
    seg_p = n_blk * blk

    def to_residue(t):
        t = t.reshape(b_, seg, dilation, h_, dh).transpose(0, 2, 3, 1, 4)
        return jnp.pad(t, ((0, 0), (0, 0), (0, 0), (0, seg_p - seg), (0, 0)))

    def neighbours(t):
        t = jnp.pad(t, ((0, 0), (0, 0), (0, 0), (blk, blk), (0, 0)))
        t = t.reshape(b_, dilation, h_, n_blk + 2, blk, dh)
        return jnp.concatenate([t[:, :, :, :-2], t[:, :, :, 1:-1], t[:, :, :, 2:]], axis=4)

    qr = to_residue(q).reshape(b_, dilation, h_, n_blk, blk, dh)
    kr = neighbours(to_residue(k))
    vr = neighbours(to_residue(v))
    qi = jnp.arange(seg_p).reshape(n_blk, blk, 1)
    kj = (jnp.arange(n_blk)[:, None, None] - 1) * blk + jnp.arange(3 * blk)[None, None, :]
    valid = (jnp.abs(qi - kj) <= half) & (kj >= 0) & (kj < seg)
    s = jnp.einsum('brhnqe,brhnke->brhnqk', qr, kr).astype(jnp.float32) * (HEAD_DIM ** -0.5)
    s = jnp.where(valid, s, NEG_INF)
    m = jnp.max(s, axis=-1, keepdims=True)
    p = jnp.exp(s - m)
    denom = jnp.sum(p, axis=-1, keepdims=True)
    o = jnp.einsum('brhnqk,brhnke->brhnqe', p, vr.astype(jnp.float32)) / denom
    lse = (m + jnp.log(denom))[..., 0]
    o = o.reshape(b_, dilation, h_, seg_p, dh)[:, :, :, :seg]
    o = o.transpose(0, 3, 1, 2, 4).reshape(b_, s_, h_, dh)
    lse = lse.reshape(b_, dilation, h_, seg_p)[:, :, :, :seg]
    lse = lse.transpose(0, 3, 1, 2).reshape(b_, s_, h_)
    return o, lse


def dilated_attention_mixer(h_qkv):
    b_, s_, _ = h_qkv.shape
    qkv = h_qkv.reshape(b_, s_, len(ATTN_GROUPS), 3, ATTN_HEADS_PER_GROUP, HEAD_DIM)
    pos = jnp.arange(s_)
    q = partial_rope(qkv[:, :, :, 0], pos)
    k = partial_rope(qkv[:, :, :, 1], pos)
    v = qkv[:, :, :, 2]
    outs, lses = [], []
    for g, (window, dilation) in enumerate(ATTN_GROUPS):
        o_g, lse_g = dilated_window_attention(q[:, :, g], k[:, :, g], v[:, :, g], window, dilation)
        outs.append(o_g)
        lses.append(lse_g)
    w = jax.nn.softmax(jnp.stack(lses, axis=0), axis=0)
    o = jnp.sum(w[..., None] * jnp.stack(outs, axis=0), axis=0)
    return o.reshape(b_, s_, ATTN_OUT_WIDTH)


def hgrn2_chunk_scan(q, f, v):
    b_, s_, h_, dk = q.shape
    dv = v.shape[-1]
    n_chunks = s_ // HGRN_CHUNK

    def chunks(t):
        return t.reshape(b_, n_chunks, HGRN_CHUNK, h_, t.shape[-1]).transpose(0, 3, 1, 2, 4)

    qc, fc, vc = chunks(q), chunks(f), chunks(v)
    kc = 1.0 - fc
    cum = jnp.cumsum(jnp.log(fc), axis=3)
    cum_last = cum[:, :, :, -1:]
    q_dec = qc * jnp.exp(cum)
    k_dec = kc * jnp.exp(-cum)
    k_end = kc * jnp.exp(cum_last - cum)
    tril = jnp.tril(jnp.ones((HGRN_CHUNK, HGRN_CHUNK), dtype=bool))
    a = jnp.where(tril, jnp.einsum('bhncd,bhnsd->bhncs', q_dec, k_dec), 0.0)
    o_intra = jnp.einsum('bhncs,bhnsv->bhncv', a, vc)
    decay = jnp.exp(cum_last[:, :, :, 0])

    def step(state, inp):
        q_n, k_n, v_n, dec_n = inp
        o_n = jnp.einsum('bhcd,bhdv->bhcv', q_n, state)
        state = dec_n[..., None] * state + jnp.einsum('bhcd,bhcv->bhdv', k_n, v_n)
        return state, o_n

    xs = (jnp.moveaxis(q_dec, 2, 0), jnp.moveaxis(k_end, 2, 0),
          jnp.moveaxis(vc, 2, 0), jnp.moveaxis(decay, 2, 0))
    _, o_inter = lax.scan(step, jnp.zeros((b_, h_, dk, dv), jnp.float32), xs)
    o = o_intra + jnp.moveaxis(o_inter, 0, 2)
    return o.transpose(0, 2, 3, 1, 4).reshape(b_, s_, h_, dv)


def bidirectional_hgrn2(hq, hf_fwd, hf_bwd, hi, hog, lb_fwd, lb_bwd, layer, norm_g):
    b_, s_, _ = hq.shape

    def heads(t):
        return t.astype(jnp.float32).reshape(b_, s_, HGRN_HEADS, HGRN_HEAD_DIM)

    def forget(hf, lb_table):
        lb = jnp.cumsum(jax.nn.softmax(lb_table.astype(jnp.float32), axis=0), axis=0)[layer]
        return heads(lb + (1.0 - lb) * jax.nn.sigmoid(hf.astype(jnp.float32)))

    q = heads(jax.nn.silu(hq.astype(jnp.float32)))
    i = heads(hi)
    f_f = forget(hf_fwd, lb_fwd)
    f_b = forget(hf_bwd, lb_bwd)
    rev = lambda t: jnp.flip(t, axis=1)
    o = hgrn2_chunk_scan(q, f_f, i) + rev(hgrn2_chunk_scan(rev(q), rev(f_b), rev(i)))
    o = o * lax.rsqrt(jnp.mean(jnp.square(o), axis=-1, keepdims=True) + LN_EPS)
    return o.reshape(b_, s_, HGRN_WIDTH) * norm_g * jax.nn.silu(hog.astype(jnp.float32))


def hybrid_mixer(h, w_in, lb_fwd, lb_bwd, layer, hgrn_norm_g, w_a, w_b, w_out):
    proj = h @ w_in
    splits = np.cumsum([HGRN_WIDTH] * 5 + [ATTN_QKV_WIDTH]).tolist()
    hq, hf_fwd, hf_bwd, hi, hog, h_qkv, h_gate = jnp.split(proj, splits, axis=-1)
    y_a = bidirectional_hgrn2(hq, hf_fwd, hf_bwd, hi, hog, lb_fwd, lb_bwd, layer,
                              hgrn_norm_g).astype(h.dtype) @ w_a
    y_b = dilated_attention_mixer(h_qkv).astype(h.dtype) @ w_b
    g_a, g_b = jnp.split(jax.nn.sigmoid(h_gate), N_BRANCHES, axis=-1)
    return (g_a * y_a + g_b * y_b) @ w_out


def setup_inputs(seed: int = 0) -> dict:
    key = jax.random.key(seed)
    ks = jax.random.split(key, 18)

    def normal(k, shape):
        return jax.random.normal(k, shape, jnp.float32)

    def dense(k, shape, scale=1.0):
        return normal(k, shape) * (shape[-2] ** -0.5) * scale

    def gain(k, shape):
        return 1.0 + 0.02 * normal(k, shape)

    def bias(k, shape):
        return 0.02 * normal(k, shape)

    return {
        'x': normal(ks[0], (BATCH, SEQ, D_MODEL)),
        'ffn1_w_in': dense(ks[1], (DEPTH, D_MODEL, 2 * D_FF)),
        'ffn1_w_out': dense(ks[2], (DEPTH, D_FF, D_MODEL), DEEPNORM_BETA),
        'ln1_g': gain(ks[3], (DEPTH, D_MODEL)),
        'ln1_b': bias(ks[4], (DEPTH, D_MODEL)),
        'mix_w_in': dense(ks[5], (DEPTH, D_MODEL, IN_COLS)),
        'hgrn_lb_fwd': 0.1 * normal(ks[6], (DEPTH + 1, HGRN_WIDTH)),
        'hgrn_lb_bwd': 0.1 * normal(ks[7], (DEPTH + 1, HGRN_WIDTH)),
        'hgrn_norm_g': gain(ks[8], (DEPTH, HGRN_WIDTH)),
        'w_branch_a': dense(ks[9], (DEPTH, HGRN_WIDTH, D_MODEL), DEEPNORM_BETA),
        'w_branch_b': dense(ks[10], (DEPTH, ATTN_OUT_WIDTH, D_MODEL), DEEPNORM_BETA),
        'mix_w_out': dense(ks[11], (DEPTH, D_MODEL, D_MODEL), DEEPNORM_BETA),
        'ln2_g': gain(ks[12], (DEPTH, D_MODEL)),
        'ln2_b': bias(ks[13], (DEPTH, D_MODEL)),
        'ffn2_w_in': dense(ks[14], (DEPTH, D_MODEL, 2 * D_FF)),
        'ffn2_w_out': dense(ks[15], (DEPTH, D_FF, D_MODEL), DEEPNORM_BETA),
        'ln3_g': gain(ks[16], (DEPTH, D_MODEL)),
        'ln3_b': bias(ks[17], (DEPTH, D_MODEL)),
    }


def reference(x, ffn1_w_in, ffn1_w_out, ln1_g, ln1_b, mix_w_in, hgrn_lb_fwd, hgrn_lb_bwd,
              hgrn_norm_g, w_branch_a, w_branch_b, mix_w_out, ln2_g, ln2_b,
              ffn2_w_in, ffn2_w_out, ln3_g, ln3_b):
    h = x
    for layer in range(DEPTH):
        h = layer_norm(DEEPNORM_ALPHA * h + 0.5 * swiglu(h, ffn1_w_in[layer], ffn1_w_out[layer]),
                       ln1_g[layer], ln1_b[layer])
        mix = hybrid_mixer(h, mix_w_in[layer], hgrn_lb_fwd, hgrn_lb_bwd, layer, hgrn_norm_g[layer],
                           w_branch_a[layer], w_branch_b[layer], mix_w_out[layer])
        h = layer_norm(DEEPNORM_ALPHA * h + mix, ln2_g[layer], ln2_b[layer])
        h = layer_norm(DEEPNORM_ALPHA * h + 0.5 * swiglu(h, ffn2_w_in[layer], ffn2_w_out[layer]),
                       ln3_g[layer], ln3_b[layer])
    return h
```

```python
import functools

import jax
import jax.numpy as jnp
import numpy as np
from jax import lax
from jax.experimental import pallas as pl
from jax.experimental.pallas import tpu as pltpu

F32 = jnp.float32
BF16 = jnp.bfloat16

HGRN_HEAD_DIM = 128
HGRN_CHUNK = 32
ATTN_GROUPS = ((128, 1), (512, 4), (2048, 16))
ATTN_HEADS_PER_GROUP = 4
HEAD_DIM = 128
ROPE_THETA = 500000.0
ROPE_DIM = HEAD_DIM // 4
LN_EPS = 1e-5
NEG_INF = -1e30

V7X_LANES = 128
V7X_VMEM_LIMIT_BYTES = 60000 * 1024


def _params(semantics):
    return pltpu.CompilerParams(dimension_semantics=semantics,
                                vmem_limit_bytes=V7X_VMEM_LIMIT_BYTES)


def _dot(a, b):
    return jnp.dot(a, b, preferred_element_type=F32)


def _dot_nt(a, b):
    return lax.dot_general(a, b, (((1,), (1,)), ((), ())), preferred_element_type=F32)


def _dot_tn(a, b):
    return lax.dot_general(a, b, (((0,), (0,)), ((), ())), preferred_element_type=F32)


def _sigmoid(x):
    return 1.0 / (1.0 + jnp.exp(-x))


def _layer_norm_rows(y, g, b):
    mu = jnp.mean(y, axis=-1, keepdims=True)
    d = y - mu
    var = jnp.mean(d * d, axis=-1, keepdims=True)
    return d * lax.rsqrt(var + LN_EPS) * g + b


def _ffn_ln_kernel(x_ref, wg_ref, wu_ref, wo_ref, g_ref, b_ref, *rest, alpha, n_f):
    (out_ref, *maybe_outbf), (acc_ref, xbf_ref) = rest[:-2], rest[-2:]
    f = pl.program_id(1)

    @pl.when(f == 0)
    def _():
        acc_ref[...] = jnp.zeros_like(acc_ref)
        xbf_ref[...] = x_ref[...].astype(BF16)

    xb = xbf_ref[...]
    gate = _dot(xb, wg_ref[...])
    up = _dot(xb, wu_ref[...])
    hid = (gate * _sigmoid(gate)) * up
    acc_ref[...] += _dot(hid.astype(BF16), wo_ref[...])

    @pl.when(f == n_f - 1)
    def _():
        y = alpha * x_ref[...] + 0.5 * acc_ref[...]
        h = _layer_norm_rows(y, g_ref[...], b_ref[...])
        out_ref[...] = h
        for outbf_ref in maybe_outbf:
            outbf_ref[...] = h.astype(BF16)


def ffn_ln(x, w_in_bf, w_out_bf, g, b, *, alpha, emit_bf16, tm=512, tf=512):
    m, d = x.shape
    ff = w_out_bf.shape[0]
    assert m % tm == 0 and ff % tf == 0
    n_f = ff // tf
    kern = functools.partial(_ffn_ln_kernel, alpha=alpha, n_f=n_f)
    out_dtypes = [F32, BF16] if emit_bf16 else [F32]
    return pl.pallas_call(
        kern,
        grid=(m // tm, n_f),
        in_specs=[
            pl.BlockSpec((tm, d), lambda i, f: (i, 0)),
            pl.BlockSpec((d, tf), lambda i, f: (0, f)),
            pl.BlockSpec((d, tf), lambda i, f: (0, f + n_f)),
            pl.BlockSpec((tf, d), lambda i, f: (f, 0)),
            pl.BlockSpec((1, d), lambda i, f: (0, 0)),
            pl.BlockSpec((1, d), lambda i, f: (0, 0)),
        ],
        out_specs=[pl.BlockSpec((tm, d), lambda i, f: (i, 0)) for _ in out_dtypes],
        out_shape=[jax.ShapeDtypeStruct((m, d), dt) for dt in out_dtypes],
        scratch_shapes=[pltpu.VMEM((tm, d), F32), pltpu.VMEM((tm, d), BF16)],
        compiler_params=_params(("parallel", "arbitrary")),
        name="ffn_ln",
    )(x, w_in_bf, w_in_bf, w_out_bf, g.reshape(1, d), b.reshape(1, d))


def _matmul_kernel(x_ref, w_ref, o_ref):
    o_ref[...] = _dot(x_ref[...], w_ref[...]).astype(o_ref.dtype)


def matmul(x_bf, w_bf, out_dtype, *, tm=1024, tn=512):
    m, k = x_bf.shape
    n = w_bf.shape[1]
    assert m % tm == 0 and n % tn == 0
    return pl.pallas_call(
        _matmul_kernel,
        grid=(m // tm, n // tn),
        in_specs=[pl.BlockSpec((tm, k), lambda i, j: (i, 0)),
                  pl.BlockSpec((k, tn), lambda i, j: (0, j))],
        out_specs=pl.BlockSpec((tm, tn), lambda i, j: (i, j)),
        out_shape=jax.ShapeDtypeStruct((m, n), out_dtype),
        compiler_params=_params(("parallel", "arbitrary")),
        name="in_proj",
    )(x_bf, w_bf)


def _split3(x):
    hi = x.astype(BF16)
    r1 = x - hi.astype(F32)
    mid = r1.astype(BF16)
    lo = (r1 - mid.astype(F32)).astype(BF16)
    return hi, mid, lo


def _hgrn_kernel(hq_ref, hff_ref, hfb_ref, hi_ref, hog_ref, lbf_ref, lbb_ref, ng_ref, out_ref,
                 qdf_ref, kdf_ref, kef_ref, qdb_ref, kdb_ref, keb_ref, decf_ref, decb_ref, o_ref,
                 *, seq, layer):
    c = HGRN_CHUNK
    slab = 4 * c
    n_slab = seq // slab
    n_chunk = seq // c

    def lower_bound(lb_ref):
        t = lb_ref[...].astype(F32)
        e = jnp.exp(t - jnp.max(t, axis=0, keepdims=True))
        sm = e / jnp.sum(e, axis=0, keepdims=True)
        return jnp.sum(sm[: layer + 1], axis=0, keepdims=True)

    lb_f = lower_bound(lbf_ref)
    lb_b = lower_bound(lbb_ref)

    ri = lax.broadcasted_iota(jnp.int32, (slab, slab), 0)
    ci = lax.broadcasted_iota(jnp.int32, (slab, slab), 1)
    same = (ri // c) == (ci // c)
    t_fwd = jnp.where(same & (ci <= ri), 1.0, 0.0).astype(BF16)
    t_bwd = jnp.where(same & (ci >= ri), 1.0, 0.0).astype(BF16)

    def prep(s, carry):
        r0 = pl.multiple_of(s * slab, slab)
        rows = pl.ds(r0, slab)
        q = hq_ref[rows, :].astype(F32)
        q = q * _sigmoid(q)

        def one_dir(hf_ref, lb, tmat, tot_row, qd_ref, kd_ref, ke_ref, dec_ref):
            f = lb + (1.0 - lb) * _sigmoid(hf_ref[rows, :].astype(F32))
            k = 1.0 - f
            lf = jnp.log(f)
            hi, mid, lo = _split3(lf)
            cum = _dot(tmat, hi) + _dot(tmat, mid) + _dot(tmat, lo)
            tots = [cum[j * c + tot_row: j * c + tot_row + 1, :] for j in range(slab // c)]
            tot = jnp.concatenate([jnp.broadcast_to(t, (c, t.shape[1])) for t in tots], axis=0)
            qd_ref[rows, :] = (q * jnp.exp(cum)).astype(BF16)
            kd_ref[rows, :] = (k * jnp.exp(-cum)).astype(BF16)
            ke_ref[rows, :] = (k * jnp.exp(tot - cum)).astype(BF16)
            dec = jnp.exp(jnp.concatenate(tots, axis=0))
            dec_ref[pl.ds(pl.multiple_of(s * 8, 8), 8), :] = jnp.concatenate([dec, dec], axis=0)

        one_dir(hff_ref, lb_f, t_fwd, c - 1, qdf_ref, kdf_ref, kef_ref, decf_ref)
        one_dir(hfb_ref, lb_b, t_bwd, 0, qdb_ref, kdb_ref, keb_ref, decb_ref)
        return carry

    lax.fori_loop(0, n_slab, prep, 0)

    rr = lax.broadcasted_iota(jnp.int32, (c, c), 0)
    cc = lax.broadcasted_iota(jnp.int32, (c, c), 1)
    tril = cc <= rr
    triu = cc >= rr

    def dec_row(dec_ref, j):
        return dec_ref[pl.ds((j // 4) * 8 + (j % 4), 1), :]

    def chunk_fwd_only(j, st_f):
        rows = pl.ds(pl.multiple_of(j * c, c), c)
        v = hi_ref[rows, :]
        qd = qdf_ref[rows, :]
        a = jnp.where(tril, _dot_nt(qd, kdf_ref[rows, :]), 0.0).astype(BF16)
        o_ref[rows, :] = _dot(a, v) + _dot_nt(qd, st_f.astype(BF16))
        return st_f * dec_row(decf_ref, j) + _dot_tn(v, kef_ref[rows, :])

    def chunk_bwd_only(jj, st_b):
        jb = n_chunk - 1 - jj
        rows_b = pl.ds(pl.multiple_of(jb * c, c), c)
        vb = hi_ref[rows_b, :]
        qdb = qdb_ref[rows_b, :]
        ab = jnp.where(triu, _dot_nt(qdb, kdb_ref[rows_b, :]), 0.0).astype(BF16)
        o_ref[rows_b, :] += _dot(ab, vb) + _dot_nt(qdb, st_b.astype(BF16))
        return st_b * dec_row(decb_ref, jb) + _dot_tn(vb, keb_ref[rows_b, :])

    zero_state = jnp.zeros((HGRN_HEAD_DIM, HGRN_HEAD_DIM), F32)
    lax.fori_loop(0, n_chunk, chunk_fwd_only, zero_state)
    lax.fori_loop(0, n_chunk, chunk_bwd_only, zero_state)

    def finish(s, carry):
        rows = pl.ds(pl.multiple_of(s * slab, slab), slab)
        o = o_ref[rows, :]
        o = o * lax.rsqrt(jnp.mean(o * o, axis=-1, keepdims=True) + LN_EPS)
        og = hog_ref[rows, :].astype(F32)
        out_ref[rows, :] = (o * ng_ref[...] * (og * _sigmoid(og))).astype(out_ref.dtype)
        return carry

    lax.fori_loop(0, n_slab, finish, 0)


def hgrn(rest, hf, lb_fwd, lb_bwd, norm_g, *, layer, n_heads, col_hq, col_hi, col_hog):
    bsz, seq, _ = rest.shape
    dh = HGRN_HEAD_DIM
    width = n_heads * dh
    nl = lb_fwd.shape[0]
    kern = functools.partial(_hgrn_kernel, seq=seq, layer=layer)

    def col(base):
        return lambda b, h: (b, 0, base // dh + h)

    return pl.pallas_call(
        kern,
        grid=(bsz, n_heads),
        in_specs=[
            pl.BlockSpec((None, seq, dh), col(col_hq)),
            pl.BlockSpec((None, seq, dh), col(0)),
            pl.BlockSpec((None, seq, dh), col(width)),
            pl.BlockSpec((None, seq, dh), col(col_hi)),
            pl.BlockSpec((None, seq, dh), col(col_hog)),
            pl.BlockSpec((nl, dh), lambda b, h: (0, h)),
            pl.BlockSpec((nl, dh), lambda b, h: (0, h)),
            pl.BlockSpec((1, dh), lambda b, h: (0, h)),
        ],
        out_specs=pl.BlockSpec((None, seq, dh), lambda b, h: (b, 0, h)),
        out_shape=jax.ShapeDtypeStruct((bsz, seq, width), BF16),
        scratch_shapes=[pltpu.VMEM((seq, dh), BF16) for _ in range(6)]
        + [pltpu.VMEM((seq // HGRN_CHUNK * 2, dh), F32) for _ in range(2)]
        + [pltpu.VMEM((seq, dh), F32)],
        compiler_params=_params(("parallel", "parallel")),
        name="hgrn",
    )(rest, hf, hf, rest, rest, lb_fwd, lb_bwd, norm_g.reshape(1, width))


ATT_TQ = 128
ATT_HALF = 64
ATT_TK = ATT_TQ + 2 * ATT_HALF


def _attn_kernel(*refs, seq):
    n_g = len(ATTN_GROUPS)
    qkv_refs = refs[: 3 * n_g]
    cos_ref, sin_ref, out_ref = refs[3 * n_g: 3 * n_g + 3]
    qr_ref, kr_ref, vf_ref, kpad_ref, vpad_ref, og_ref, lse_ref = refs[3 * n_g + 3:]
    dh = HEAD_DIM
    scale = HEAD_DIM ** -0.5

    lane = lax.broadcasted_iota(jnp.int32, (seq, dh), 1)
    cosf = cos_ref[...]
    sinf = sin_ref[...]

    def rope(t):
        swapped = jnp.where(lane < ROPE_DIM // 2,
                            pltpu.roll(t, dh - ROPE_DIM // 2, 1),
                            pltpu.roll(t, ROPE_DIM // 2, 1))
        return t * cosf + swapped * sinf

    qi_rel = lax.broadcasted_iota(jnp.int32, (ATT_TQ, ATT_TK), 0)
    kj_rel = lax.broadcasted_iota(jnp.int32, (ATT_TQ, ATT_TK), 1) - ATT_HALF
    band = jnp.abs(qi_rel - kj_rel) <= ATT_HALF

    for g, (window, dil) in enumerate(ATTN_GROUPS):
        assert window // (2 * dil) == ATT_HALF
        seg = seq // dil
        n_blk = seg // ATT_TQ
        q_ref, k_ref, v_ref = qkv_refs[3 * g: 3 * g + 3]
        qr_ref[...] = rope(q_ref[...].astype(F32))
        kr_ref[...] = rope(k_ref[...].astype(F32))
        vf_ref[...] = v_ref[...].astype(F32)
        zpad = jnp.zeros((ATT_HALF, dh), BF16)
        zpad2 = jnp.zeros((ATT_TK - ATT_HALF, dh), BF16)
        kpad_ref[pl.ds(0, ATT_HALF), :] = zpad
        vpad_ref[pl.ds(0, ATT_HALF), :] = zpad
        kpad_ref[pl.ds(ATT_HALF + seg, ATT_TK - ATT_HALF), :] = zpad2
        vpad_ref[pl.ds(ATT_HALF + seg, ATT_TK - ATT_HALF), :] = zpad2

        for cls in range(dil):
            if dil == 1:
                cls_rows = pl.ds(0, seg)
            else:
                cls_rows = pl.ds(cls, seg, stride=dil)
            kpad_ref[pl.ds(ATT_HALF, seg), :] = kr_ref[cls_rows, :].astype(BF16)
            vpad_ref[pl.ds(ATT_HALF, seg), :] = vf_ref[cls_rows, :].astype(BF16)

            def block(i, carry, cls=cls, dil=dil, seg=seg):
                q0 = pl.multiple_of(i * ATT_TQ, ATT_TQ)
                if dil == 1:
                    tok_rows = pl.ds(q0, ATT_TQ)
                else:
                    tok_rows = pl.ds(cls + dil * q0, ATT_TQ, stride=dil)
                qb = qr_ref[tok_rows, :].astype(BF16)
                kb = kpad_ref[pl.ds(q0, ATT_TK), :]
                vb = vpad_ref[pl.ds(q0, ATT_TK), :]
                s = _dot_nt(qb, kb) * scale
                kj = kj_rel + q0
                valid = band & (kj >= 0) & (kj < seg)
                s = jnp.where(valid, s, NEG_INF)
                m = jnp.max(s, axis=-1, keepdims=True)
                p = jnp.exp(s - m)
                denom = jnp.sum(p, axis=-1, keepdims=True)
                o = _dot(p.astype(BF16), vb) / denom
                lse = m + jnp.log(denom)
                og_ref[g, tok_rows, :] = o
                lse_ref[g, tok_rows, :] = jnp.broadcast_to(lse, (ATT_TQ, dh))
                return carry

            lax.fori_loop(0, n_blk, block, 0)

    def combine(s, carry):
        rows = pl.ds(pl.multiple_of(s * ATT_TK, ATT_TK), ATT_TK)
        lses = [lse_ref[g, rows, :] for g in range(n_g)]
        mx = functools.reduce(jnp.maximum, lses)
        ws = [jnp.exp(l - mx) for l in lses]
        wsum = functools.reduce(lambda a, b: a + b, ws)
        acc = ws[0] * og_ref[0, rows, :]
        for g in range(1, n_g):
            acc = acc + ws[g] * og_ref[g, rows, :]
        out_ref[rows, :] = (acc / wsum).astype(out_ref.dtype)
        return carry

    lax.fori_loop(0, seq // ATT_TK, combine, 0)


def _rope_tables(seq):
    inv_freq = ROPE_THETA ** (-jnp.arange(0, ROPE_DIM, 2, dtype=F32) / ROPE_DIM)
    ang = jnp.arange(seq).astype(F32)[:, None] * inv_freq
    cos, sin = jnp.cos(ang), jnp.sin(ang)
    pad1 = jnp.ones((seq, HEAD_DIM - ROPE_DIM), F32)
    pad0 = jnp.zeros((seq, HEAD_DIM - ROPE_DIM), F32)
    return (jnp.concatenate([cos, cos, pad1], axis=1),
            jnp.concatenate([-sin, sin, pad0], axis=1))


def attn(rest, *, col_qkv):
    bsz, seq, _ = rest.shape
    dh = HEAD_DIM
    hpg = ATTN_HEADS_PER_GROUP
    n_g = len(ATTN_GROUPS)
    assert seq % (max(d for _, d in ATTN_GROUPS) * ATT_TQ) == 0
    cosf, sinf = _rope_tables(seq)

    def col(g, t):
        base = col_qkv // dh + (g * 3 + t) * hpg
        return lambda b, j: (b, 0, base + j)

    in_specs = [pl.BlockSpec((None, seq, dh), col(g, t)) for g in range(n_g) for t in range(3)]
    in_specs += [pl.BlockSpec((seq, dh), lambda b, j: (0, 0)) for _ in range(2)]
    max_seg = seq
    return pl.pallas_call(
        functools.partial(_attn_kernel, seq=seq),
        grid=(bsz, hpg),
        in_specs=in_specs,
        out_specs=pl.BlockSpec((None, seq, dh), lambda b, j: (b, 0, j)),
        out_shape=jax.ShapeDtypeStruct((bsz, seq, hpg * dh), BF16),
        scratch_shapes=[
            pltpu.VMEM((seq, dh), F32), pltpu.VMEM((seq, dh), F32), pltpu.VMEM((seq, dh), F32),
            pltpu.VMEM((max_seg + ATT_TK, dh), BF16), pltpu.VMEM((max_seg + ATT_TK, dh), BF16),
            pltpu.VMEM((n_g, seq, dh), F32), pltpu.VMEM((n_g, seq, dh), F32),
        ],
        compiler_params=_params(("parallel", "parallel")),
        name="attn",
    )(*([rest] * (3 * n_g)), cosf, sinf)


def _mix_out_kernel(oa_ref, ob_ref, ga_ref, gb_ref, h_ref, wa_ref, wb_ref, wo_ref, g_ref, b_ref,
                    out_ref, *, alpha):
    ya = _dot(oa_ref[...], wa_ref[...])
    yb = _dot(ob_ref[...], wb_ref[...])
    z = _sigmoid(ga_ref[...].astype(F32)) * ya + _sigmoid(gb_ref[...].astype(F32)) * yb
    mix = _dot(z.astype(BF16), wo_ref[...])
    out_ref[...] = _layer_norm_rows(alpha * h_ref[...] + mix, g_ref[...], b_ref[...])


def mix_out(oa, ob, rest2d, h, wa_bf, wb_bf, wo_bf, g, b, *, alpha, col_gate, tm=256):
    m, d = h.shape
    wa_w = oa.shape[1]
    wb_w = ob.shape[1]
    assert m % tm == 0 and col_gate % d == 0
    const = lambda i: (0, 0)
    single = pl.Buffered(1)
    return pl.pallas_call(
        functools.partial(_mix_out_kernel, alpha=alpha),
        grid=(m // tm,),
        in_specs=[
            pl.BlockSpec((tm, wa_w), lambda i: (i, 0)),
            pl.BlockSpec((tm, wb_w), lambda i: (i, 0)),
            pl.BlockSpec((tm, d), lambda i: (i, col_gate // d)),
            pl.BlockSpec((tm, d), lambda i: (i, col_gate // d + 1)),
            pl.BlockSpec((tm, d), lambda i: (i, 0)),
            pl.BlockSpec((wa_w, d), const, pipeline_mode=single),
            pl.BlockSpec((wb_w, d), const, pipeline_mode=single),
            pl.BlockSpec((d, d), const, pipeline_mode=single),
            pl.BlockSpec((1, d), const),
            pl.BlockSpec((1, d), const),
        ],
        out_specs=pl.BlockSpec((tm, d), lambda i: (i, 0)),
        out_shape=jax.ShapeDtypeStruct((m, d), F32),
        compiler_params=_params(("parallel",)),
        name="mix_out",
    )(oa, ob, rest2d, rest2d, h, wa_bf, wb_bf, wo_bf, g.reshape(1, d), b.reshape(1, d))


def kernel(x, ffn1_w_in, ffn1_w_out, ln1_g, ln1_b, mix_w_in, hgrn_lb_fwd, hgrn_lb_bwd, hgrn_norm_g,
           w_branch_a, w_branch_b, mix_w_out, ln2_g, ln2_b, ffn2_w_in, ffn2_w_out, ln3_g, ln3_b):
    bsz, seq, d = x.shape
    depth = ffn1_w_in.shape[0]
    alpha = (2.0 * depth) ** 0.25
    hw = w_branch_a.shape[1]
    n_heads = hw // HGRN_HEAD_DIM
    qkv_w = len(ATTN_GROUPS) * 3 * ATTN_HEADS_PER_GROUP * HEAD_DIM
    m = bsz * seq

    h = x.reshape(m, d)
    for layer in range(depth):
        h, h_bf = ffn_ln(h, ffn1_w_in[layer].astype(BF16), ffn1_w_out[layer].astype(BF16),
                         ln1_g[layer], ln1_b[layer], alpha=alpha, emit_bf16=True)

        w_in = mix_w_in[layer]
        w_hf = w_in[:, hw:3 * hw].astype(BF16)
        w_rest = jnp.concatenate([w_in[:, :hw], w_in[:, 3 * hw:5 * hw], w_in[:, 5 * hw + qkv_w:],
                                  w_in[:, 5 * hw:5 * hw + qkv_w]], axis=1).astype(BF16)
        hf = matmul(h_bf, w_hf, F32).reshape(bsz, seq, 2 * hw)
        rest2d = matmul(h_bf, w_rest, BF16)
        rest = rest2d.reshape(bsz, seq, -1)
        col_hq, col_hi, col_hog, col_gate = 0, hw, 2 * hw, 3 * hw
        col_qkv = 3 * hw + 2 * d

        o_a = hgrn(rest, hf, hgrn_lb_fwd, hgrn_lb_bwd, hgrn_norm_g[layer], layer=layer,
                   n_heads=n_heads, col_hq=col_hq, col_hi=col_hi, col_hog=col_hog)
        o_b = attn(rest, col_qkv=col_qkv)

        h = mix_out(o_a.reshape(m, hw), o_b.reshape(m, -1), rest2d, h,
                    w_branch_a[layer].astype(BF16), w_branch_b[layer].astype(BF16),
                    mix_w_out[layer].astype(BF16), ln2_g[layer], ln2_b[layer],
                    alpha=alpha, col_gate=col_gate)

        (h,) = ffn_ln(h, ffn2_w_in[layer].astype(BF16), ffn2_w_out[layer].astype(BF16),
                      ln3_g[layer], ln3_b[layer], alpha=alpha, emit_bf16=False)
    return h.reshape(bsz, seq, d)
```

```python
import functools

import jax
import jax.numpy as jnp
import numpy as np
from jax import lax
from jax.experimental import pallas as pl
from jax.experimental.pallas import tpu as pltpu

F32 = jnp.float32
BF16 = jnp.bfloat16

HGRN_HEAD_DIM = 128
HGRN_CHUNK = 32
HGRN_SLAB = 4 * HGRN_CHUNK
ATTN_GROUPS = ((128, 1), (512, 4), (2048, 16))
ATTN_HEADS_PER_GROUP = 4
HEAD_DIM = 128
ROPE_THETA = 500000.0
ROPE_DIM = HEAD_DIM // 4
LN_EPS = 1e-5
NEG_INF = -1e30

V7X_LANES = 128
V7X_VMEM_LIMIT_BYTES = 60000 * 1024


def _params(semantics):
    return pltpu.CompilerParams(dimension_semantics=semantics,
                                vmem_limit_bytes=V7X_VMEM_LIMIT_BYTES)


def _dot(a, b):
    return jnp.dot(a, b, preferred_element_type=F32)


def _dot_nt(a, b):
    return lax.dot_general(a, b, (((1,), (1,)), ((), ())), preferred_element_type=F32)


def _dot_tn(a, b):
    return lax.dot_general(a, b, (((0,), (0,)), ((), ())), preferred_element_type=F32)


def _sigmoid(x):
    return 1.0 / (1.0 + jnp.exp(-x))


def _layer_norm_rows(y, g, b):
    mu = jnp.mean(y, axis=-1, keepdims=True)
    d = y - mu
    var = jnp.mean(d * d, axis=-1, keepdims=True)
    return d * lax.rsqrt(var + LN_EPS) * g + b


def _ffn_ln_kernel(x_ref, wg_ref, wu_ref, wo_ref, g_ref, b_ref, *rest, alpha, n_f):
    (out_ref, *maybe_outbf), (acc_ref, xbf_ref) = rest[:-2], rest[-2:]
    f = pl.program_id(1)

    @pl.when(f == 0)
    def _():
        acc_ref[...] = jnp.zeros_like(acc_ref)
        xbf_ref[...] = x_ref[...].astype(BF16)

    xb = xbf_ref[...]
    gate = _dot(xb, wg_ref[...])
    up = _dot(xb, wu_ref[...])
    hid = (gate * _sigmoid(gate)) * up
    acc_ref[...] += _dot(hid.astype(BF16), wo_ref[...])

    @pl.when(f == n_f - 1)
    def _():
        y = alpha * x_ref[...] + 0.5 * acc_ref[...]
        h = _layer_norm_rows(y, g_ref[...], b_ref[...])
        out_ref[...] = h
        for outbf_ref in maybe_outbf:
            outbf_ref[...] = h.astype(BF16)


def ffn_ln(x, w_in_bf, w_out_bf, g, b, *, alpha, emit_bf16, tm=512, tf=512):
    m, d = x.shape
    ff = w_out_bf.shape[0]
    assert m % tm == 0 and ff % tf == 0
    n_f = ff // tf
    kern = functools.partial(_ffn_ln_kernel, alpha=alpha, n_f=n_f)
    out_dtypes = [F32, BF16] if emit_bf16 else [F32]
    return pl.pallas_call(
        kern,
        grid=(m // tm, n_f),
        in_specs=[
            pl.BlockSpec((tm, d), lambda i, f: (i, 0)),
            pl.BlockSpec((d, tf), lambda i, f: (0, f)),
            pl.BlockSpec((d, tf), lambda i, f: (0, f + n_f)),
            pl.BlockSpec((tf, d), lambda i, f: (f, 0)),
            pl.BlockSpec((1, d), lambda i, f: (0, 0)),
            pl.BlockSpec((1, d), lambda i, f: (0, 0)),
        ],
        out_specs=[pl.BlockSpec((tm, d), lambda i, f: (i, 0)) for _ in out_dtypes],
        out_shape=[jax.ShapeDtypeStruct((m, d), dt) for dt in out_dtypes],
        scratch_shapes=[pltpu.VMEM((tm, d), F32), pltpu.VMEM((tm, d), BF16)],
        compiler_params=_params(("parallel", "arbitrary")),
        name="ffn_ln",
    )(x, w_in_bf, w_in_bf, w_out_bf, g.reshape(1, d), b.reshape(1, d))


def _matmul_kernel(x_ref, w_ref, o_ref):
    o_ref[...] = _dot(x_ref[...], w_ref[...]).astype(o_ref.dtype)


def matmul(x_bf, w_bf, out_dtype, *, tm=1024, tn=512):
    m, k = x_bf.shape
    n = w_bf.shape[1]
    assert m % tm == 0 and n % tn == 0
    return pl.pallas_call(
        _matmul_kernel,
        grid=(m // tm, n // tn),
        in_specs=[pl.BlockSpec((tm, k), lambda i, j: (i, 0)),
                  pl.BlockSpec((k, tn), lambda i, j: (0, j))],
        out_specs=pl.BlockSpec((tm, tn), lambda i, j: (i, j)),
        out_shape=jax.ShapeDtypeStruct((m, n), out_dtype),
        compiler_params=_params(("parallel", "arbitrary")),
        name="in_proj",
    )(x_bf, w_bf)


def _split3(x):
    hi = x.astype(BF16)
    r1 = x - hi.astype(F32)
    mid = r1.astype(BF16)
    lo = (r1 - mid.astype(F32)).astype(BF16)
    return hi, mid, lo


def _hgrn_kernel(hq_ref, hff_ref, hfb_ref, hi_ref, hog_ref, lbf_ref, lbb_ref, ng_ref, out_ref,
                 qd_ref, kk_ref, q64_ref, k64_ref, q128_ref, dec_ref, kv_ref, snap_ref, *, seq, layer):
    c = HGRN_CHUNK
    slab = HGRN_SLAB
    n_slab = seq // slab
    dh = HGRN_HEAD_DIM
    assert slab == 4 * c

    def lower_bound(lb_ref):
        t = lb_ref[...].astype(F32)
        e = jnp.exp(t - jnp.max(t, axis=0, keepdims=True))
        sm = e / jnp.sum(e, axis=0, keepdims=True)
        return jnp.sum(sm[: layer + 1], axis=0, keepdims=True)

    lbs = (lower_bound(lbf_ref), lower_bound(lbb_ref))
    hf_refs = (hff_ref, hfb_ref)

    ri = lax.broadcasted_iota(jnp.int32, (slab, slab), 0)
    ci = lax.broadcasted_iota(jnp.int32, (slab, slab), 1)
    cr = ri // c
    cs = ci // c
    same = cr == cs
    m_diag = (same & (ci <= ri), same & (ci >= ri))
    m_adj = ((cr == cs + 1) & (cr % 2 == 1), (cr + 1 == cs) & (cr % 2 == 0))
    m_64 = ((cr >= 2) & (cs < 2), (cr < 2) & (cs >= 2))
    tmats = tuple(jnp.where(m, 1.0, 0.0).astype(BF16) for m in m_diag)

    def prep(s, carry):
        r0 = pl.multiple_of(s * slab, slab)
        rows = pl.ds(r0, slab)
        q = hq_ref[rows, :].astype(F32)
        q = q * _sigmoid(q)
        k128 = []
        for d in range(2):
            f = lbs[d] + (1.0 - lbs[d]) * _sigmoid(hf_refs[d][rows, :].astype(F32))
            k = 1.0 - f
            hi, mid, lo = _split3(jnp.log(f))
            cum = _dot(tmats[d], hi) + _dot(tmats[d], mid) + _dot(tmats[d], lo)
            tot_row = c - 1 if d == 0 else 0
            tot = [cum[j * c + tot_row: j * c + tot_row + 1, :] for j in range(4)]
            e = [jnp.exp(t) for t in tot]
            f0, f1 = e[0] * e[1], e[2] * e[3]
            if d == 0:
                qmul64, kmul64 = (None, e[0], None, e[2]), (e[1], None, e[3], None)
                qmul128, kmul128 = (None, None, f0, f0), (f1, f1, None, None)
            else:
                qmul64, kmul64 = (e[1], None, e[3], None), (None, e[0], None, e[2])
                qmul128, kmul128 = (f1, f1, None, None), (None, None, f0, f0)
            mul = lambda x, m: x if m is None else x * m
            pieces = []
            for j in range(4):
                sl = slice(j * c, (j + 1) * c)
                rows_j = pl.ds(r0 + j * c, c)
                cj, kj = cum[sl], k[sl]
                qd = q[sl] * jnp.exp(cj)
                ke = kj * jnp.exp(tot[j] - cj)
                q64 = mul(qd, qmul64[j])
                k64 = mul(ke, kmul64[j])
                qd_ref[d, rows_j, :] = qd.astype(BF16)
                kk_ref[d, s, pl.ds(j * c, c), :] = (kj * jnp.exp(-cj)).astype(BF16)
                kk_ref[d, s, pl.ds(slab + j * c, c), :] = ke.astype(BF16)
                q64_ref[d, rows_j, :] = q64.astype(BF16)
                k64_ref[d, rows_j, :] = k64.astype(BF16)
                q128_ref[rows_j, d * dh:(d + 1) * dh] = mul(q64, qmul128[j]).astype(BF16)
                pieces.append(mul(k64, kmul128[j]).astype(BF16))
            k128.append(jnp.concatenate(pieces, axis=0))
            dec_ref[d, pl.ds(pl.multiple_of(s * 8, 8), 8), :] = jnp.broadcast_to(f0 * f1, (8, dh))
        kv_ref[s] = _dot_tn(hi_ref[rows, :], jnp.concatenate(k128, axis=1))
        return carry

    lax.fori_loop(0, n_slab, prep, 0)

    def state_step(i, st, d):
        s = i if d == 0 else n_slab - 1 - i
        snap_ref[s, :, d * dh:(d + 1) * dh] = st.astype(BF16)
        return st * dec_ref[d, pl.ds(s * 8, 1), :] + kv_ref[s, :, d * dh:(d + 1) * dh]

    zero_state = jnp.zeros((dh, dh), F32)
    for d in range(2):
        lax.fori_loop(0, n_slab, functools.partial(state_step, d=d), zero_state)

    def out_slab(s, carry):
        rows = pl.ds(pl.multiple_of(s * slab, slab), slab)
        a = None
        for d in range(2):
            m1 = _dot_nt(qd_ref[d, rows, :], kk_ref[d, s])
            m2 = _dot_nt(q64_ref[d, rows, :], k64_ref[d, rows, :])
            ad = jnp.where(m_diag[d], m1[:, :slab],
                           jnp.where(m_adj[d], m1[:, slab:], jnp.where(m_64[d], m2, 0.0)))
            a = ad if a is None else a + ad
        o = _dot(a.astype(BF16), hi_ref[rows, :]) + _dot_nt(q128_ref[rows, :], snap_ref[s])
        o = o * lax.rsqrt(jnp.mean(o * o, axis=-1, keepdims=True) + LN_EPS)
        og = hog_ref[rows, :].astype(F32)
        out_ref[rows, :] = (o * ng_ref[...] * (og * _sigmoid(og))).astype(out_ref.dtype)
        return carry

    lax.fori_loop(0, n_slab, out_slab, 0, unroll=2)


def hgrn(rest, hf, lb_fwd, lb_bwd, norm_g, *, layer, n_heads, col_hq, col_hi, col_hog):
    bsz, seq, _ = rest.shape
    dh = HGRN_HEAD_DIM
    width = n_heads * dh
    nl = lb_fwd.shape[0]
    slab = HGRN_SLAB
    assert seq % slab == 0
    n_slab = seq // slab
    kern = functools.partial(_hgrn_kernel, seq=seq, layer=layer)

    def col(base):
        return lambda b, h: (b, 0, base // dh + h)

    return pl.pallas_call(
        kern,
        grid=(bsz, n_heads),
        in_specs=[
            pl.BlockSpec((None, seq, dh), col(col_hq)),
            pl.BlockSpec((None, seq, dh), col(0)),
            pl.BlockSpec((None, seq, dh), col(width)),
            pl.BlockSpec((None, seq, dh), col(col_hi)),
            pl.BlockSpec((None, seq, dh), col(col_hog)),
            pl.BlockSpec((nl, dh), lambda b, h: (0, h)),
            pl.BlockSpec((nl, dh), lambda b, h: (0, h)),
            pl.BlockSpec((1, dh), lambda b, h: (0, h)),
        ],
        out_specs=pl.BlockSpec((None, seq, dh), lambda b, h: (b, 0, h)),
        out_shape=jax.ShapeDtypeStruct((bsz, seq, width), BF16),
        scratch_shapes=[
            pltpu.VMEM((2, seq, dh), BF16),
            pltpu.VMEM((2, n_slab, 2 * slab, dh), BF16),
            pltpu.VMEM((2, seq, dh), BF16),
            pltpu.VMEM((2, seq, dh), BF16),
            pltpu.VMEM((seq, 2 * dh), BF16),
            pltpu.VMEM((2, n_slab * 8, dh), F32),
            pltpu.VMEM((n_slab, dh, 2 * dh), F32),
            pltpu.VMEM((n_slab, dh, 2 * dh), BF16),
        ],
        compiler_params=_params(("parallel", "parallel")),
        name="hgrn",
    )(rest, hf, hf, rest, rest, lb_fwd, lb_bwd, norm_g.reshape(1, width))


ATT_TQ = 128
ATT_HALF = 64
ATT_TK = ATT_TQ + 2 * ATT_HALF


def _attn_kernel(*refs, seq):
    n_g = len(ATTN_GROUPS)
    qkv_refs = refs[: 3 * n_g]
    cos_ref, sin_ref, out_ref = refs[3 * n_g: 3 * n_g + 3]
    qr_ref, kr_ref, vf_ref, kpad_ref, vpad_ref, og_ref, lse_ref = refs[3 * n_g + 3:]
    dh = HEAD_DIM
    scale = HEAD_DIM ** -0.5

    lane = lax.broadcasted_iota(jnp.int32, (seq, dh), 1)
    cosf = cos_ref[...]
    sinf = sin_ref[...]

    def rope(t):
        swapped = jnp.where(lane < ROPE_DIM // 2,
                            pltpu.roll(t, dh - ROPE_DIM // 2, 1),
                            pltpu.roll(t, ROPE_DIM // 2, 1))
        return t * cosf + swapped * sinf

    qi_rel = lax.broadcasted_iota(jnp.int32, (ATT_TQ, ATT_TK), 0)
    kj_rel = lax.broadcasted_iota(jnp.int32, (ATT_TQ, ATT_TK), 1) - ATT_HALF
    band = jnp.abs(qi_rel - kj_rel) <= ATT_HALF

    for g, (window, dil) in enumerate(ATTN_GROUPS):
        assert window // (2 * dil) == ATT_HALF
        seg = seq // dil
        n_blk = seg // ATT_TQ
        q_ref, k_ref, v_ref = qkv_refs[3 * g: 3 * g + 3]
        qr_ref[...] = rope(q_ref[...].astype(F32))
        kr_ref[...] = rope(k_ref[...].astype(F32))
        vf_ref[...] = v_ref[...].astype(F32)
        zpad = jnp.zeros((ATT_HALF, dh), BF16)
        zpad2 = jnp.zeros((ATT_TK - ATT_HALF, dh), BF16)
        kpad_ref[pl.ds(0, ATT_HALF), :] = zpad
        vpad_ref[pl.ds(0, ATT_HALF), :] = zpad
        kpad_ref[pl.ds(ATT_HALF + seg, ATT_TK - ATT_HALF), :] = zpad2
        vpad_ref[pl.ds(ATT_HALF + seg, ATT_TK - ATT_HALF), :] = zpad2

        for cls in range(dil):
            if dil == 1:
                cls_rows = pl.ds(0, seg)
            else:
                cls_rows = pl.ds(cls, seg, stride=dil)
            kpad_ref[pl.ds(ATT_HALF, seg), :] = kr_ref[cls_rows, :].astype(BF16)
            vpad_ref[pl.ds(ATT_HALF, seg), :] = vf_ref[cls_rows, :].astype(BF16)

            def block(i, carry, cls=cls, dil=dil, seg=seg):
                q0 = pl.multiple_of(i * ATT_TQ, ATT_TQ)
                if dil == 1:
                    tok_rows = pl.ds(q0, ATT_TQ)
                else:
                    tok_rows = pl.ds(cls + dil * q0, ATT_TQ, stride=dil)
                qb = qr_ref[tok_rows, :].astype(BF16)
                kb = kpad_ref[pl.ds(q0, ATT_TK), :]
                vb = vpad_ref[pl.ds(q0, ATT_TK), :]
                s = _dot_nt(qb, kb) * scale
                kj = kj_rel + q0
                valid = band & (kj >= 0) & (kj < seg)
                s = jnp.where(valid, s, NEG_INF)
                m = jnp.max(s, axis=-1, keepdims=True)
                p = jnp.exp(s - m)
                denom = jnp.sum(p, axis=-1, keepdims=True)
                o = _dot(p.astype(BF16), vb) / denom
                lse = m + jnp.log(denom)
                og_ref[g, tok_rows, :] = o
                lse_ref[g, tok_rows, :] = jnp.broadcast_to(lse, (ATT_TQ, dh))
                return carry

            lax.fori_loop(0, n_blk, block, 0)

    def combine(s, carry):
        rows = pl.ds(pl.multiple_of(s * ATT_TK, ATT_TK), ATT_TK)
        lses = [lse_ref[g, rows, :] for g in range(n_g)]
        mx = functools.reduce(jnp.maximum, lses)
        ws = [jnp.exp(l - mx) for l in lses]
        wsum = functools.reduce(lambda a, b: a + b, ws)
        acc = ws[0] * og_ref[0, rows, :]
        for g in range(1, n_g):
            acc = acc + ws[g] * og_ref[g, rows, :]
        out_ref[rows, :] = (acc / wsum).astype(out_ref.dtype)
        return carry

    lax.fori_loop(0, seq // ATT_TK, combine, 0)


def _rope_tables(seq):
    inv_freq = ROPE_THETA ** (-jnp.arange(0, ROPE_DIM, 2, dtype=F32) / ROPE_DIM)
    ang = jnp.arange(seq).astype(F32)[:, None] * inv_freq
    cos, sin = jnp.cos(ang), jnp.sin(ang)
    pad1 = jnp.ones((seq, HEAD_DIM - ROPE_DIM), F32)
    pad0 = jnp.zeros((seq, HEAD_DIM - ROPE_DIM), F32)
    return (jnp.concatenate([cos, cos, pad1], axis=1),
            jnp.concatenate([-sin, sin, pad0], axis=1))


def attn(rest, *, col_qkv):
    bsz, seq, _ = rest.shape
    dh = HEAD_DIM
    hpg = ATTN_HEADS_PER_GROUP
    n_g = len(ATTN_GROUPS)
    assert seq % (max(d for _, d in ATTN_GROUPS) * ATT_TQ) == 0
    cosf, sinf = _rope_tables(seq)

    def col(g, t):
        base = col_qkv // dh + (g * 3 + t) * hpg
        return lambda b, j: (b, 0, base + j)

    in_specs = [pl.BlockSpec((None, seq, dh), col(g, t)) for g in range(n_g) for t in range(3)]
    in_specs += [pl.BlockSpec((seq, dh), lambda b, j: (0, 0)) for _ in range(2)]
    max_seg = seq
    return pl.pallas_call(
        functools.partial(_attn_kernel, seq=seq),
        grid=(bsz, hpg),
        in_specs=in_specs,
        out_specs=pl.BlockSpec((None, seq, dh), lambda b, j: (b, 0, j)),
        out_shape=jax.ShapeDtypeStruct((bsz, seq, hpg * dh), BF16),
        scratch_shapes=[
            pltpu.VMEM((seq, dh), F32), pltpu.VMEM((seq, dh), F32), pltpu.VMEM((seq, dh), F32),
            pltpu.VMEM((max_seg + ATT_TK, dh), BF16), pltpu.VMEM((max_seg + ATT_TK, dh), BF16),
            pltpu.VMEM((n_g, seq, dh), F32), pltpu.VMEM((n_g, seq, dh), F32),
        ],
        compiler_params=_params(("parallel", "parallel")),
        name="attn",
    )(*([rest] * (3 * n_g)), cosf, sinf)


def _mix_out_kernel(oa_ref, ob_ref, ga_ref, gb_ref, h_ref, wa_ref, wb_ref, wo_ref, g_ref, b_ref,
                    out_ref, *, alpha):
    ya = _dot(oa_ref[...], wa_ref[...])
    yb = _dot(ob_ref[...], wb_ref[...])
    z = _sigmoid(ga_ref[...].astype(F32)) * ya + _sigmoid(gb_ref[...].astype(F32)) * yb
    mix = _dot(z.astype(BF16), wo_ref[...])
    out_ref[...] = _layer_norm_rows(alpha * h_ref[...] + mix, g_ref[...], b_ref[...])


def mix_out(oa, ob, rest2d, h, wa_bf, wb_bf, wo_bf, g, b, *, alpha, col_gate, tm=256):
    m, d = h.shape
    wa_w = oa.shape[1]
    wb_w = ob.shape[1]
    assert m % tm == 0 and col_gate % d == 0
    const = lambda i: (0, 0)
    single = pl.Buffered(1)
    return pl.pallas_call(
        functools.partial(_mix_out_kernel, alpha=alpha),
        grid=(m // tm,),
        in_specs=[
            pl.BlockSpec((tm, wa_w), lambda i: (i, 0)),
            pl.BlockSpec((tm, wb_w), lambda i: (i, 0)),
            pl.BlockSpec((tm, d), lambda i: (i, col_gate // d)),
            pl.BlockSpec((tm, d), lambda i: (i, col_gate // d + 1)),
            pl.BlockSpec((tm, d), lambda i: (i, 0)),
            pl.BlockSpec((wa_w, d), const, pipeline_mode=single),
            pl.BlockSpec((wb_w, d), const, pipeline_mode=single),
            pl.BlockSpec((d, d), const, pipeline_mode=single),
            pl.BlockSpec((1, d), const),
            pl.BlockSpec((1, d), const),
        ],
        out_specs=pl.BlockSpec((tm, d), lambda i: (i, 0)),
        out_shape=jax.ShapeDtypeStruct((m, d), F32),
        compiler_params=_params(("parallel",)),
        name="mix_out",
    )(oa, ob, rest2d, rest2d, h, wa_bf, wb_bf, wo_bf, g.reshape(1, d), b.reshape(1, d))


def kernel(x, ffn1_w_in, ffn1_w_out, ln1_g, ln1_b, mix_w_in, hgrn_lb_fwd, hgrn_lb_bwd, hgrn_norm_g,
           w_branch_a, w_branch_b, mix_w_out, ln2_g, ln2_b, ffn2_w_in, ffn2_w_out, ln3_g, ln3_b):
    bsz, seq, d = x.shape
    depth = ffn1_w_in.shape[0]
    alpha = (2.0 * depth) ** 0.25
    hw = w_branch_a.shape[1]
    n_heads = hw // HGRN_HEAD_DIM
    qkv_w = len(ATTN_GROUPS) * 3 * ATTN_HEADS_PER_GROUP * HEAD_DIM
    m = bsz * seq

    h = x.reshape(m, d)
    for layer in range(depth):
        h, h_bf = ffn_ln(h, ffn1_w_in[layer].astype(BF16), ffn1_w_out[layer].astype(BF16),
                         ln1_g[layer], ln1_b[layer], alpha=alpha, emit_bf16=True)

        w_in = mix_w_in[layer]
        w_hf = w_in[:, hw:3 * hw].astype(BF16)
        w_rest = jnp.concatenate([w_in[:, :hw], w_in[:, 3 * hw:5 * hw], w_in[:, 5 * hw + qkv_w:],
                                  w_in[:, 5 * hw:5 * hw + qkv_w]], axis=1).astype(BF16)
        hf = matmul(h_bf, w_hf, F32).reshape(bsz, seq, 2 * hw)
        rest2d = matmul(h_bf, w_rest, BF16)
        rest = rest2d.reshape(bsz, seq, -1)
        col_hq, col_hi, col_hog, col_gate = 0, hw, 2 * hw, 3 * hw
        col_qkv = 3 * hw + 2 * d

        o_a = hgrn(rest, hf, hgrn_lb_fwd, hgrn_lb_bwd, hgrn_norm_g[layer], layer=layer,
                   n_heads=n_heads, col_hq=col_hq, col_hi=col_hi, col_hog=col_hog)
        o_b = attn(rest, col_qkv=col_qkv)

        h = mix_out(o_a.reshape(m, hw), o_b.reshape(m, -1), rest2d, h,
                    w_branch_a[layer].astype(BF16), w_branch_b[layer].astype(BF16),
                    mix_w_out[layer].astype(BF16), ln2_g[layer], ln2_b[layer],
                    alpha=alpha, col_gate=col_gate)

        (h,) = ffn_ln(h, ffn2_w_in[layer].astype(BF16), ffn2_w_out[layer].astype(BF16),
                      ln3_g[layer], ln3_b[layer], alpha=alpha, emit_bf16=False)
    return h.reshape(bsz, seq, d)
```

```python
import functools

import jax
import jax.numpy as jnp
import numpy as np
from jax import lax
from jax.experimental import pallas as pl
from jax.experimental.pallas import tpu as pltpu

F32 = jnp.float32
BF16 = jnp.bfloat16

HGRN_HEAD_DIM = 128
HGRN_CHUNK = 32
HGRN_SLAB = 4 * HGRN_CHUNK
ATTN_GROUPS = ((128, 1), (512, 4), (2048, 16))
ATTN_HEADS_PER_GROUP = 4
HEAD_DIM = 128
ROPE_THETA = 500000.0
ROPE_DIM = HEAD_DIM // 4
LN_EPS = 1e-5
NEG_INF = -1e30

V7X_LANES = 128
V7X_VMEM_LIMIT_BYTES = 60000 * 1024


def _params(semantics):
    return pltpu.CompilerParams(dimension_semantics=semantics,
                                vmem_limit_bytes=V7X_VMEM_LIMIT_BYTES)


def _dot(a, b):
    return jnp.dot(a, b, preferred_element_type=F32)


def _dot_nt(a, b):
    return lax.dot_general(a, b, (((1,), (1,)), ((), ())), preferred_element_type=F32)


def _dot_tn(a, b):
    return lax.dot_general(a, b, (((0,), (0,)), ((), ())), preferred_element_type=F32)


def _sigmoid(x):
    return 1.0 / (1.0 + jnp.exp(-x))


def _layer_norm_rows(y, g, b):
    mu = jnp.mean(y, axis=-1, keepdims=True)
    d = y - mu
    var = jnp.mean(d * d, axis=-1, keepdims=True)
    return d * lax.rsqrt(var + LN_EPS) * g + b


def _ffn_ln_kernel(x_ref, wg_ref, wu_ref, wo_ref, g_ref, b_ref, *rest, alpha, n_f):
    (out_ref, *maybe_outbf), (acc_ref, xbf_ref) = rest[:-2], rest[-2:]
    f = pl.program_id(1)

    @pl.when(f == 0)
    def _():
        acc_ref[...] = jnp.zeros_like(acc_ref)
        xbf_ref[...] = x_ref[...].astype(BF16)

    xb = xbf_ref[...]
    gate = _dot(xb, wg_ref[...])
    up = _dot(xb, wu_ref[...])
    hid = (gate * _sigmoid(gate)) * up
    acc_ref[...] += _dot(hid.astype(BF16), wo_ref[...])

    @pl.when(f == n_f - 1)
    def _():
        y = alpha * x_ref[...] + 0.5 * acc_ref[...]
        h = _layer_norm_rows(y, g_ref[...], b_ref[...])
        out_ref[...] = h
        for outbf_ref in maybe_outbf:
            outbf_ref[...] = h.astype(BF16)


def ffn_ln(x, w_in_bf, w_out_bf, g, b, *, alpha, emit_bf16, tm=512, tf=512):
    m, d = x.shape
    ff = w_out_bf.shape[0]
    assert m % tm == 0 and ff % tf == 0
    n_f = ff // tf
    kern = functools.partial(_ffn_ln_kernel, alpha=alpha, n_f=n_f)
    out_dtypes = [F32, BF16] if emit_bf16 else [F32]
    return pl.pallas_call(
        kern,
        grid=(m // tm, n_f),
        in_specs=[
            pl.BlockSpec((tm, d), lambda i, f: (i, 0)),
            pl.BlockSpec((d, tf), lambda i, f: (0, f)),
            pl.BlockSpec((d, tf), lambda i, f: (0, f + n_f)),
            pl.BlockSpec((tf, d), lambda i, f: (f, 0)),
            pl.BlockSpec((1, d), lambda i, f: (0, 0)),
            pl.BlockSpec((1, d), lambda i, f: (0, 0)),
        ],
        out_specs=[pl.BlockSpec((tm, d), lambda i, f: (i, 0)) for _ in out_dtypes],
        out_shape=[jax.ShapeDtypeStruct((m, d), dt) for dt in out_dtypes],
        scratch_shapes=[pltpu.VMEM((tm, d), F32), pltpu.VMEM((tm, d), BF16)],
        compiler_params=_params(("parallel", "arbitrary")),
        name="ffn_ln",
    )(x, w_in_bf, w_in_bf, w_out_bf, g.reshape(1, d), b.reshape(1, d))


def _matmul_kernel(x_ref, w_ref, o_ref):
    o_ref[...] = _dot(x_ref[...], w_ref[...]).astype(o_ref.dtype)


def matmul(x_bf, w_bf, out_dtype, *, tm=2048, tn=512):
    m, k = x_bf.shape
    n = w_bf.shape[1]
    assert m % tm == 0 and n % tn == 0
    return pl.pallas_call(
        _matmul_kernel,
        grid=(m // tm, n // tn),
        in_specs=[pl.BlockSpec((tm, k), lambda i, j: (i, 0)),
                  pl.BlockSpec((k, tn), lambda i, j: (0, j))],
        out_specs=pl.BlockSpec((tm, tn), lambda i, j: (i, j)),
        out_shape=jax.ShapeDtypeStruct((m, n), out_dtype),
        compiler_params=_params(("parallel", "arbitrary")),
        name="in_proj",
    )(x_bf, w_bf)


def _split3(x):
    hi = x.astype(BF16)
    r1 = x - hi.astype(F32)
    mid = r1.astype(BF16)
    lo = (r1 - mid.astype(F32)).astype(BF16)
    return hi, mid, lo


def _hgrn_kernel(hq_ref, hff_ref, hfb_ref, hi_ref, hog_ref, lbf_ref, lbb_ref, ng_ref, out_ref,
                 qd_ref, kk_ref, q64_ref, k64_ref, q128_ref, dec_ref, kv_ref, snap_ref, *, seq, layer):
    c = HGRN_CHUNK
    slab = HGRN_SLAB
    n_slab = seq // slab
    dh = HGRN_HEAD_DIM
    assert slab == 4 * c

    def lower_bound(lb_ref):
        t = lb_ref[...].astype(F32)
        e = jnp.exp(t - jnp.max(t, axis=0, keepdims=True))
        sm = e / jnp.sum(e, axis=0, keepdims=True)
        return jnp.sum(sm[: layer + 1], axis=0, keepdims=True)

    lbs = (lower_bound(lbf_ref), lower_bound(lbb_ref))
    hf_refs = (hff_ref, hfb_ref)

    ri = lax.broadcasted_iota(jnp.int32, (slab, slab), 0)
    ci = lax.broadcasted_iota(jnp.int32, (slab, slab), 1)
    cr = ri // c
    cs = ci // c
    same = cr == cs
    m_diag = (same & (ci <= ri), same & (ci >= ri))
    m_adj = ((cr == cs + 1) & (cr % 2 == 1), (cr + 1 == cs) & (cr % 2 == 0))
    m_64 = ((cr >= 2) & (cs < 2), (cr < 2) & (cs >= 2))
    tmats = tuple(jnp.where(m, 1.0, 0.0).astype(BF16) for m in m_diag)

    def prep(s, carry):
        r0 = pl.multiple_of(s * slab, slab)
        rows = pl.ds(r0, slab)
        q = hq_ref[rows, :].astype(F32)
        q = q * _sigmoid(q)
        k128 = []
        for d in range(2):
            f = lbs[d] + (1.0 - lbs[d]) * _sigmoid(hf_refs[d][rows, :].astype(F32))
            k = 1.0 - f
            hi, mid, lo = _split3(jnp.log(f))
            cum = _dot(tmats[d], hi) + _dot(tmats[d], mid) + _dot(tmats[d], lo)
            tot_row = c - 1 if d == 0 else 0
            tot = [cum[j * c + tot_row: j * c + tot_row + 1, :] for j in range(4)]
            e = [jnp.exp(t) for t in tot]
            f0, f1 = e[0] * e[1], e[2] * e[3]
            if d == 0:
                qmul64, kmul64 = (None, e[0], None, e[2]), (e[1], None, e[3], None)
                qmul128, kmul128 = (None, None, f0, f0), (f1, f1, None, None)
            else:
                qmul64, kmul64 = (e[1], None, e[3], None), (None, e[0], None, e[2])
                qmul128, kmul128 = (f1, f1, None, None), (None, None, f0, f0)
            mul = lambda x, m: x if m is None else x * m
            pieces = []
            for j in range(4):
                sl = slice(j * c, (j + 1) * c)
                rows_j = pl.ds(r0 + j * c, c)
                cj, kj = cum[sl], k[sl]
                qd = q[sl] * jnp.exp(cj)
                ke = kj * jnp.exp(tot[j] - cj)
                q64 = mul(qd, qmul64[j])
                k64 = mul(ke, kmul64[j])
                qd_ref[d, rows_j, :] = qd.astype(BF16)
                kk_ref[d, s, pl.ds(j * c, c), :] = (kj * jnp.exp(-cj)).astype(BF16)
                kk_ref[d, s, pl.ds(slab + j * c, c), :] = ke.astype(BF16)
                q64_ref[d, rows_j, :] = q64.astype(BF16)
                k64_ref[d, rows_j, :] = k64.astype(BF16)
                q128_ref[rows_j, d * dh:(d + 1) * dh] = mul(q64, qmul128[j]).astype(BF16)
                pieces.append(mul(k64, kmul128[j]).astype(BF16))
            k128.append(jnp.concatenate(pieces, axis=0))
            dec_ref[d, pl.ds(pl.multiple_of(s * 8, 8), 8), :] = jnp.broadcast_to(f0 * f1, (8, dh))
        kv_ref[s] = _dot_tn(hi_ref[rows, :], jnp.concatenate(k128, axis=1))
        return carry

    lax.fori_loop(0, n_slab, prep, 0, unroll=4)

    def state_step(i, st, d):
        s = i if d == 0 else n_slab - 1 - i
        snap_ref[s, :, d * dh:(d + 1) * dh] = st.astype(BF16)
        return st * dec_ref[d, pl.ds(s * 8, 1), :] + kv_ref[s, :, d * dh:(d + 1) * dh]

    zero_state = jnp.zeros((dh, dh), F32)
    for d in range(2):
        lax.fori_loop(0, n_slab, functools.partial(state_step, d=d), zero_state)

    def out_slab(s, carry):
        rows = pl.ds(pl.multiple_of(s * slab, slab), slab)
        a = None
        for d in range(2):
            m1 = _dot_nt(qd_ref[d, rows, :], kk_ref[d, s])
            m2 = _dot_nt(q64_ref[d, rows, :], k64_ref[d, rows, :])
            ad = jnp.where(m_diag[d], m1[:, :slab],
                           jnp.where(m_adj[d], m1[:, slab:], jnp.where(m_64[d], m2, 0.0)))
            a = ad if a is None else a + ad
        o = _dot(a.astype(BF16), hi_ref[rows, :]) + _dot_nt(q128_ref[rows, :], snap_ref[s])
        o = o * lax.rsqrt(jnp.mean(o * o, axis=-1, keepdims=True) + LN_EPS)
        og = hog_ref[rows, :].astype(F32)
        out_ref[rows, :] = (o * ng_ref[...] * (og * _sigmoid(og))).astype(out_ref.dtype)
        return carry

    lax.fori_loop(0, n_slab, out_slab, 0, unroll=8)


def hgrn(rest, hf, lb_fwd, lb_bwd, norm_g, *, layer, n_heads, col_hq, col_hi, col_hog):
    bsz, seq, _ = rest.shape
    dh = HGRN_HEAD_DIM
    width = n_heads * dh
    nl = lb_fwd.shape[0]
    slab = HGRN_SLAB
    assert seq % slab == 0
    n_slab = seq // slab
    kern = functools.partial(_hgrn_kernel, seq=seq, layer=layer)

    def col(base):
        return lambda b, h: (b, 0, base // dh + h)

    return pl.pallas_call(
        kern,
        grid=(bsz, n_heads),
        in_specs=[
            pl.BlockSpec((None, seq, dh), col(col_hq)),
            pl.BlockSpec((None, seq, dh), col(0)),
            pl.BlockSpec((None, seq, dh), col(width)),
            pl.BlockSpec((None, seq, dh), col(col_hi)),
            pl.BlockSpec((None, seq, dh), col(col_hog)),
            pl.BlockSpec((nl, dh), lambda b, h: (0, h)),
            pl.BlockSpec((nl, dh), lambda b, h: (0, h)),
            pl.BlockSpec((1, dh), lambda b, h: (0, h)),
        ],
        out_specs=pl.BlockSpec((None, seq, dh), lambda b, h: (b, 0, h)),
        out_shape=jax.ShapeDtypeStruct((bsz, seq, width), BF16),
        scratch_shapes=[
            pltpu.VMEM((2, seq, dh), BF16),
            pltpu.VMEM((2, n_slab, 2 * slab, dh), BF16),
            pltpu.VMEM((2, seq, dh), BF16),
            pltpu.VMEM((2, seq, dh), BF16),
            pltpu.VMEM((seq, 2 * dh), BF16),
            pltpu.VMEM((2, n_slab * 8, dh), F32),
            pltpu.VMEM((n_slab, dh, 2 * dh), F32),
            pltpu.VMEM((n_slab, dh, 2 * dh), BF16),
        ],
        compiler_params=_params(("parallel", "parallel")),
        name="hgrn",
    )(rest, hf, hf, rest, rest, lb_fwd, lb_bwd, norm_g.reshape(1, width))


ATT_TQ = 128
ATT_HALF = 64
ATT_TK = ATT_TQ + 2 * ATT_HALF


def _attn_kernel(*refs, seq):
    n_g = len(ATTN_GROUPS)
    qkv_refs = refs[: 3 * n_g]
    cos_ref, sin_ref, out_ref = refs[3 * n_g: 3 * n_g + 3]
    qr_ref, kr_ref, vf_ref, kpad_ref, vpad_ref, og_ref, lse_ref = refs[3 * n_g + 3:]
    dh = HEAD_DIM
    scale = HEAD_DIM ** -0.5

    lane = lax.broadcasted_iota(jnp.int32, (seq, dh), 1)
    cosf = cos_ref[...]
    sinf = sin_ref[...]

    def rope(t):
        swapped = jnp.where(lane < ROPE_DIM // 2,
                            pltpu.roll(t, dh - ROPE_DIM // 2, 1),
                            pltpu.roll(t, ROPE_DIM // 2, 1))
        return t * cosf + swapped * sinf

    qi_rel = lax.broadcasted_iota(jnp.int32, (ATT_TQ, ATT_TK), 0)
    kj_rel = lax.broadcasted_iota(jnp.int32, (ATT_TQ, ATT_TK), 1) - ATT_HALF
    band = jnp.abs(qi_rel - kj_rel) <= ATT_HALF

    for g, (window, dil) in enumerate(ATTN_GROUPS):
        assert window // (2 * dil) == ATT_HALF
        seg = seq // dil
        n_blk = seg // ATT_TQ
        q_ref, k_ref, v_ref = qkv_refs[3 * g: 3 * g + 3]
        qr_ref[...] = rope(q_ref[...].astype(F32))
        kr_ref[...] = rope(k_ref[...].astype(F32))
        vf_ref[...] = v_ref[...].astype(F32)
        zpad = jnp.zeros((ATT_HALF, dh), BF16)
        zpad2 = jnp.zeros((ATT_TK - ATT_HALF, dh), BF16)
        kpad_ref[pl.ds(0, ATT_HALF), :] = zpad
        vpad_ref[pl.ds(0, ATT_HALF), :] = zpad
        kpad_ref[pl.ds(ATT_HALF + seg, ATT_TK - ATT_HALF), :] = zpad2
        vpad_ref[pl.ds(ATT_HALF + seg, ATT_TK - ATT_HALF), :] = zpad2

        for cls in range(dil):
            if dil == 1:
                cls_rows = pl.ds(0, seg)
            else:
                cls_rows = pl.ds(cls, seg, stride=dil)
            kpad_ref[pl.ds(ATT_HALF, seg), :] = kr_ref[cls_rows, :].astype(BF16)
            vpad_ref[pl.ds(ATT_HALF, seg), :] = vf_ref[cls_rows, :].astype(BF16)

            def block(i, carry, cls=cls, dil=dil, seg=seg):
                q0 = pl.multiple_of(i * ATT_TQ, ATT_TQ)
                if dil == 1:
                    tok_rows = pl.ds(q0, ATT_TQ)
                else:
                    tok_rows = pl.ds(cls + dil * q0, ATT_TQ, stride=dil)
                qb = qr_ref[tok_rows, :].astype(BF16)
                kb = kpad_ref[pl.ds(q0, ATT_TK), :]
                vb = vpad_ref[pl.ds(q0, ATT_TK), :]
                s = _dot_nt(qb, kb) * scale
                kj = kj_rel + q0
                valid = band & (kj >= 0) & (kj < seg)
                s = jnp.where(valid, s, NEG_INF)
                m = jnp.max(s, axis=-1, keepdims=True)
                p = jnp.exp(s - m)
                denom = jnp.sum(p, axis=-1, keepdims=True)
                o = _dot(p.astype(BF16), vb) / denom
                lse = m + jnp.log(denom)
                og_ref[g, tok_rows, :] = o
                lse_ref[g, tok_rows, :] = jnp.broadcast_to(lse, (ATT_TQ, dh))
                return carry

            lax.fori_loop(0, n_blk, block, 0)

    def combine(s, carry):
        rows = pl.ds(pl.multiple_of(s * ATT_TK, ATT_TK), ATT_TK)
        lses = [lse_ref[g, rows, :] for g in range(n_g)]
        mx = functools.reduce(jnp.maximum, lses)
        ws = [jnp.exp(l - mx) for l in lses]
        wsum = functools.reduce(lambda a, b: a + b, ws)
        acc = ws[0] * og_ref[0, rows, :]
        for g in range(1, n_g):
            acc = acc + ws[g] * og_ref[g, rows, :]
        out_ref[rows, :] = (acc / wsum).astype(out_ref.dtype)
        return carry

    lax.fori_loop(0, seq // ATT_TK, combine, 0)


def _rope_tables(seq):
    inv_freq = ROPE_THETA ** (-jnp.arange(0, ROPE_DIM, 2, dtype=F32) / ROPE_DIM)
    ang = jnp.arange(seq).astype(F32)[:, None] * inv_freq
    cos, sin = jnp.cos(ang), jnp.sin(ang)
    pad1 = jnp.ones((seq, HEAD_DIM - ROPE_DIM), F32)
    pad0 = jnp.zeros((seq, HEAD_DIM - ROPE_DIM), F32)
    return (jnp.concatenate([cos, cos, pad1], axis=1),
            jnp.concatenate([-sin, sin, pad0], axis=1))


def attn(rest, *, col_qkv):
    bsz, seq, _ = rest.shape
    dh = HEAD_DIM
    hpg = ATTN_HEADS_PER_GROUP
    n_g = len(ATTN_GROUPS)
    assert seq % (max(d for _, d in ATTN_GROUPS) * ATT_TQ) == 0
    cosf, sinf = _rope_tables(seq)

    def col(g, t):
        base = col_qkv // dh + (g * 3 + t) * hpg
        return lambda b, j: (b, 0, base + j)

    in_specs = [pl.BlockSpec((None, seq, dh), col(g, t)) for g in range(n_g) for t in range(3)]
    in_specs += [pl.BlockSpec((seq, dh), lambda b, j: (0, 0)) for _ in range(2)]
    max_seg = seq
    return pl.pallas_call(
        functools.partial(_attn_kernel, seq=seq),
        grid=(bsz, hpg),
        in_specs=in_specs,
        out_specs=pl.BlockSpec((None, seq, dh), lambda b, j: (b, 0, j)),
        out_shape=jax.ShapeDtypeStruct((bsz, seq, hpg * dh), BF16),
        scratch_shapes=[
            pltpu.VMEM((seq, dh), F32), pltpu.VMEM((seq, dh), F32), pltpu.VMEM((seq, dh), F32),
            pltpu.VMEM((max_seg + ATT_TK, dh), BF16), pltpu.VMEM((max_seg + ATT_TK, dh), BF16),
            pltpu.VMEM((n_g, seq, dh), F32), pltpu.VMEM((n_g, seq, dh), F32),
        ],
        compiler_params=_params(("parallel", "parallel")),
        name="attn",
    )(*([rest] * (3 * n_g)), cosf, sinf)


def _mix_out_kernel(oa_ref, ob_ref, ga_ref, gb_ref, h_ref, wa_ref, wb_ref, wo_ref, g_ref, b_ref,
                    out_ref, *, alpha):
    ya = _dot(oa_ref[...], wa_ref[...])
    yb = _dot(ob_ref[...], wb_ref[...])
    z = _sigmoid(ga_ref[...].astype(F32)) * ya + _sigmoid(gb_ref[...].astype(F32)) * yb
    mix = _dot(z.astype(BF16), wo_ref[...])
    out_ref[...] = _layer_norm_rows(alpha * h_ref[...] + mix, g_ref[...], b_ref[...])


def mix_out(oa, ob, rest2d, h, wa_bf, wb_bf, wo_bf, g, b, *, alpha, col_gate, tm=256):
    m, d = h.shape
    wa_w = oa.shape[1]
    wb_w = ob.shape[1]
    assert m % tm == 0 and col_gate % d == 0
    const = lambda i: (0, 0)
    single = pl.Buffered(1)
    return pl.pallas_call(
        functools.partial(_mix_out_kernel, alpha=alpha),
        grid=(m // tm,),
        in_specs=[
            pl.BlockSpec((tm, wa_w), lambda i: (i, 0)),
            pl.BlockSpec((tm, wb_w), lambda i: (i, 0)),
            pl.BlockSpec((tm, d), lambda i: (i, col_gate // d)),
            pl.BlockSpec((tm, d), lambda i: (i, col_gate // d + 1)),
            pl.BlockSpec((tm, d), lambda i: (i, 0)),
            pl.BlockSpec((wa_w, d), const, pipeline_mode=single),
            pl.BlockSpec((wb_w, d), const, pipeline_mode=single),
            pl.BlockSpec((d, d), const, pipeline_mode=single),
            pl.BlockSpec((1, d), const),
            pl.BlockSpec((1, d), const),
        ],
        out_specs=pl.BlockSpec((tm, d), lambda i: (i, 0)),
        out_shape=jax.ShapeDtypeStruct((m, d), F32),
        compiler_params=_params(("parallel",)),
        name="mix_out",
    )(oa, ob, rest2d, rest2d, h, wa_bf, wb_bf, wo_bf, g.reshape(1, d), b.reshape(1, d))


def kernel(x, ffn1_w_in, ffn1_w_out, ln1_g, ln1_b, mix_w_in, hgrn_lb_fwd, hgrn_lb_bwd, hgrn_norm_g,
           w_branch_a, w_branch_b, mix_w_out, ln2_g, ln2_b, ffn2_w_in, ffn2_w_out, ln3_g, ln3_b):
    bsz, seq, d = x.shape
    depth = ffn1_w_in.shape[0]
    alpha = (2.0 * depth) ** 0.25
    hw = w_branch_a.shape[1]
    n_heads = hw // HGRN_HEAD_DIM
    qkv_w = len(ATTN_GROUPS) * 3 * ATTN_HEADS_PER_GROUP * HEAD_DIM
    m = bsz * seq

    h = x.reshape(m, d)
    for layer in range(depth):
        h, h_bf = ffn_ln(h, ffn1_w_in[layer].astype(BF16), ffn1_w_out[layer].astype(BF16),
                         ln1_g[layer], ln1_b[layer], alpha=alpha, emit_bf16=True)

        w_in = mix_w_in[layer]
        w_hf = w_in[:, hw:3 * hw].astype(BF16)
        w_rest = jnp.concatenate([w_in[:, :hw], w_in[:, 3 * hw:5 * hw], w_in[:, 5 * hw + qkv_w:],
                                  w_in[:, 5 * hw:5 * hw + qkv_w]], axis=1).astype(BF16)
        hf = matmul(h_bf, w_hf, F32).reshape(bsz, seq, 2 * hw)
        rest2d = matmul(h_bf, w_rest, BF16)
        rest = rest2d.reshape(bsz, seq, -1)
        col_hq, col_hi, col_hog, col_gate = 0, hw, 2 * hw, 3 * hw
        col_qkv = 3 * hw + 2 * d

        o_a = hgrn(rest, hf, hgrn_lb_fwd, hgrn_lb_bwd, hgrn_norm_g[layer], layer=layer,
                   n_heads=n_heads, col_hq=col_hq, col_hi=col_hi, col_hog=col_hog)
        o_b = attn(rest, col_qkv=col_qkv)

        h = mix_out(o_a.reshape(m, hw), o_b.reshape(m, -1), rest2d, h,
                    w_branch_a[layer].astype(BF16), w_branch_b[layer].astype(BF16),
                    mix_w_out[layer].astype(BF16), ln2_g[layer], ln2_b[layer],
                    alpha=alpha, col_gate=col_gate)

        (h,) = ffn_ln(h, ffn2_w_in[layer].astype(BF16), ffn2_w_out[layer].astype(BF16),
                      ln3_g[layer], ln3_b[layer], alpha=alpha, emit_bf16=False)
    return h.reshape(bsz, seq, d)
```

```python
import functools

import jax
import jax.numpy as jnp
import numpy as np
from jax import lax
from jax.experimental import pallas as pl
from jax.experimental.pallas import tpu as pltpu

F32 = jnp.float32
BF16 = jnp.bfloat16

HGRN_HEAD_DIM = 128
HGRN_CHUNK = 32
HGRN_SLAB = 4 * HGRN_CHUNK
ATTN_GROUPS = ((128, 1), (512, 4), (2048, 16))
ATTN_HEADS_PER_GROUP = 4
HEAD_DIM = 128
ROPE_THETA = 500000.0
ROPE_DIM = HEAD_DIM // 4
LN_EPS = 1e-5
NEG_INF = -1e30

V7X_LANES = 128
V7X_VMEM_LIMIT_BYTES = 60000 * 1024


def _params(semantics):
    return pltpu.CompilerParams(dimension_semantics=semantics,
                                vmem_limit_bytes=V7X_VMEM_LIMIT_BYTES)


def _dot(a, b):
    return jnp.dot(a, b, preferred_element_type=F32)


def _dot_nt(a, b):
    return lax.dot_general(a, b, (((1,), (1,)), ((), ())), preferred_element_type=F32)


def _dot_tn(a, b):
    return lax.dot_general(a, b, (((0,), (0,)), ((), ())), preferred_element_type=F32)


def _sigmoid(x):
    return 1.0 / (1.0 + jnp.exp(-x))


def _layer_norm_rows(y, g, b):
    mu = jnp.mean(y, axis=-1, keepdims=True)
    d = y - mu
    var = jnp.mean(d * d, axis=-1, keepdims=True)
    return d * lax.rsqrt(var + LN_EPS) * g + b


def _ffn_ln_kernel(x_ref, wg_ref, wu_ref, wo_ref, g_ref, b_ref, *rest, alpha, n_f):
    (out_ref, *maybe_outbf), (acc_ref, xbf_ref) = rest[:-2], rest[-2:]
    f = pl.program_id(1)

    @pl.when(f == 0)
    def _():
        acc_ref[...] = jnp.zeros_like(acc_ref)
        xbf_ref[...] = x_ref[...].astype(BF16)

    xb = xbf_ref[...]
    gate = _dot(xb, wg_ref[...])
    up = _dot(xb, wu_ref[...])
    hid = (gate * _sigmoid(gate)) * up
    acc_ref[...] += _dot(hid.astype(BF16), wo_ref[...])

    @pl.when(f == n_f - 1)
    def _():
        y = alpha * x_ref[...] + 0.5 * acc_ref[...]
        h = _layer_norm_rows(y, g_ref[...], b_ref[...])
        out_ref[...] = h
        for outbf_ref in maybe_outbf:
            outbf_ref[...] = h.astype(BF16)


def ffn_ln(x, w_in_bf, w_out_bf, g, b, *, alpha, emit_bf16, tm=512, tf=512):
    m, d = x.shape
    ff = w_out_bf.shape[0]
    assert m % tm == 0 and ff % tf == 0
    n_f = ff // tf
    kern = functools.partial(_ffn_ln_kernel, alpha=alpha, n_f=n_f)
    out_dtypes = [F32, BF16] if emit_bf16 else [F32]
    return pl.pallas_call(
        kern,
        grid=(m // tm, n_f),
        in_specs=[
            pl.BlockSpec((tm, d), lambda i, f: (i, 0)),
            pl.BlockSpec((d, tf), lambda i, f: (0, f)),
            pl.BlockSpec((d, tf), lambda i, f: (0, f + n_f)),
            pl.BlockSpec((tf, d), lambda i, f: (f, 0)),
            pl.BlockSpec((1, d), lambda i, f: (0, 0)),
            pl.BlockSpec((1, d), lambda i, f: (0, 0)),
        ],
        out_specs=[pl.BlockSpec((tm, d), lambda i, f: (i, 0)) for _ in out_dtypes],
        out_shape=[jax.ShapeDtypeStruct((m, d), dt) for dt in out_dtypes],
        scratch_shapes=[pltpu.VMEM((tm, d), F32), pltpu.VMEM((tm, d), BF16)],
        compiler_params=_params(("parallel", "arbitrary")),
        name="ffn_ln",
    )(x, w_in_bf, w_in_bf, w_out_bf, g.reshape(1, d), b.reshape(1, d))


def _matmul_kernel(x_ref, w_ref, o_ref):
    o_ref[...] = _dot(x_ref[...], w_ref[...]).astype(o_ref.dtype)


def matmul(x_bf, w_bf, out_dtype, *, tm=2048, tn=512):
    m, k = x_bf.shape
    n = w_bf.shape[1]
    assert m % tm == 0 and n % tn == 0
    return pl.pallas_call(
        _matmul_kernel,
        grid=(m // tm, n // tn),
        in_specs=[pl.BlockSpec((tm, k), lambda i, j: (i, 0)),
                  pl.BlockSpec((k, tn), lambda i, j: (0, j))],
        out_specs=pl.BlockSpec((tm, tn), lambda i, j: (i, j)),
        out_shape=jax.ShapeDtypeStruct((m, n), out_dtype),
        compiler_params=_params(("parallel", "arbitrary")),
        name="in_proj",
    )(x_bf, w_bf)


def _split3(x):
    hi = x.astype(BF16)
    r1 = x - hi.astype(F32)
    mid = r1.astype(BF16)
    lo = (r1 - mid.astype(F32)).astype(BF16)
    return hi, mid, lo


def _hgrn_kernel(hq_ref, hff_ref, hfb_ref, hi_ref, hog_ref, lbf_ref, lbb_ref, ng_ref, out_ref,
                 qd_ref, kk_ref, q64_ref, k64_ref, q128_ref, dec_ref, kv_ref, snap_ref, *, seq, layer):
    c = HGRN_CHUNK
    slab = HGRN_SLAB
    n_slab = seq // slab
    dh = HGRN_HEAD_DIM
    assert slab == 4 * c

    def lower_bound(lb_ref):
        t = lb_ref[...].astype(F32)
        e = jnp.exp(t - jnp.max(t, axis=0, keepdims=True))
        sm = e / jnp.sum(e, axis=0, keepdims=True)
        return jnp.sum(sm[: layer + 1], axis=0, keepdims=True)

    lbs = (lower_bound(lbf_ref), lower_bound(lbb_ref))
    hf_refs = (hff_ref, hfb_ref)

    ri = lax.broadcasted_iota(jnp.int32, (slab, slab), 0)
    ci = lax.broadcasted_iota(jnp.int32, (slab, slab), 1)
    cr = ri // c
    cs = ci // c
    same = cr == cs
    m_diag = (same & (ci <= ri), same & (ci >= ri))
    m_adj = ((cr == cs + 1) & (cr % 2 == 1), (cr + 1 == cs) & (cr % 2 == 0))
    m_64 = ((cr >= 2) & (cs < 2), (cr < 2) & (cs >= 2))
    tmats = tuple(jnp.where(m, 1.0, 0.0).astype(BF16) for m in m_diag)

    def prep(s, carry):
        r0 = pl.multiple_of(s * slab, slab)
        rows = pl.ds(r0, slab)
        q = hq_ref[rows, :].astype(F32)
        q = q * _sigmoid(q)
        k128 = []
        for d in range(2):
            f = lbs[d] + (1.0 - lbs[d]) * _sigmoid(hf_refs[d][rows, :].astype(F32))
            k = 1.0 - f
            hi, mid, lo = _split3(jnp.log(f))
            cum = _dot(tmats[d], hi) + _dot(tmats[d], mid) + _dot(tmats[d], lo)
            tot_row = c - 1 if d == 0 else 0
            tot = [cum[j * c + tot_row: j * c + tot_row + 1, :] for j in range(4)]
            e = [jnp.exp(t) for t in tot]
            f0, f1 = e[0] * e[1], e[2] * e[3]
            if d == 0:
                qmul64, kmul64 = (None, e[0], None, e[2]), (e[1], None, e[3], None)
                qmul128, kmul128 = (None, None, f0, f0), (f1, f1, None, None)
            else:
                qmul64, kmul64 = (e[1], None, e[3], None), (None, e[0], None, e[2])
                qmul128, kmul128 = (f1, f1, None, None), (None, None, f0, f0)
            mul = lambda x, m: x if m is None else x * m
            pieces = []
            for j in range(4):
                sl = slice(j * c, (j + 1) * c)
                rows_j = pl.ds(r0 + j * c, c)
                cj, kj = cum[sl], k[sl]
                qd = q[sl] * jnp.exp(cj)
                ke = kj * jnp.exp(tot[j] - cj)
                q64 = mul(qd, qmul64[j])
                k64 = mul(ke, kmul64[j])
                qd_ref[d, rows_j, :] = qd.astype(BF16)
                kk_ref[d, s, pl.ds(j * c, c), :] = (kj * jnp.exp(-cj)).astype(BF16)
                kk_ref[d, s, pl.ds(slab + j * c, c), :] = ke.astype(BF16)
                q64_ref[d, rows_j, :] = q64.astype(BF16)
                k64_ref[d, rows_j, :] = k64.astype(BF16)
                q128_ref[rows_j, d * dh:(d + 1) * dh] = mul(q64, qmul128[j]).astype(BF16)
                pieces.append(mul(k64, kmul128[j]).astype(BF16))
            k128.append(jnp.concatenate(pieces, axis=0))
            dec_ref[d, pl.ds(pl.multiple_of(s * 8, 8), 8), :] = jnp.broadcast_to(f0 * f1, (8, dh))
        kv_ref[s] = _dot_tn(hi_ref[rows, :], jnp.concatenate(k128, axis=1))
        return carry

    lax.fori_loop(0, n_slab, prep, 0, unroll=4)

    def state_step(i, st, d):
        s = i if d == 0 else n_slab - 1 - i
        snap_ref[s, :, d * dh:(d + 1) * dh] = st.astype(BF16)
        return st * dec_ref[d, pl.ds(s * 8, 1), :] + kv_ref[s, :, d * dh:(d + 1) * dh]

    zero_state = jnp.zeros((dh, dh), F32)
    for d in range(2):
        lax.fori_loop(0, n_slab, functools.partial(state_step, d=d), zero_state)

    def out_slab(s, carry):
        rows = pl.ds(pl.multiple_of(s * slab, slab), slab)
        a = None
        for d in range(2):
            m1 = _dot_nt(qd_ref[d, rows, :], kk_ref[d, s])
            m2 = _dot_nt(q64_ref[d, rows, :], k64_ref[d, rows, :])
            ad = jnp.where(m_diag[d], m1[:, :slab],
                           jnp.where(m_adj[d], m1[:, slab:], jnp.where(m_64[d], m2, 0.0)))
            a = ad if a is None else a + ad
        o = _dot(a.astype(BF16), hi_ref[rows, :]) + _dot_nt(q128_ref[rows, :], snap_ref[s])
        o = o * lax.rsqrt(jnp.mean(o * o, axis=-1, keepdims=True) + LN_EPS)
        og = hog_ref[rows, :].astype(F32)
        out_ref[rows, :] = (o * ng_ref[...] * (og * _sigmoid(og))).astype(out_ref.dtype)
        return carry

    lax.fori_loop(0, n_slab, out_slab, 0, unroll=8)


def hgrn(rest, hf, lb_fwd, lb_bwd, norm_g, *, layer, n_heads, col_hq, col_hi, col_hog):
    bsz, seq, _ = rest.shape
    dh = HGRN_HEAD_DIM
    width = n_heads * dh
    nl = lb_fwd.shape[0]
    slab = HGRN_SLAB
    assert seq % slab == 0
    n_slab = seq // slab
    kern = functools.partial(_hgrn_kernel, seq=seq, layer=layer)

    def col(base):
        return lambda b, h: (b, 0, base // dh + h)

    return pl.pallas_call(
        kern,
        grid=(bsz, n_heads),
        in_specs=[
            pl.BlockSpec((None, seq, dh), col(col_hq)),
            pl.BlockSpec((None, seq, dh), col(0)),
            pl.BlockSpec((None, seq, dh), col(width)),
            pl.BlockSpec((None, seq, dh), col(col_hi)),
            pl.BlockSpec((None, seq, dh), col(col_hog)),
            pl.BlockSpec((nl, dh), lambda b, h: (0, h)),
            pl.BlockSpec((nl, dh), lambda b, h: (0, h)),
            pl.BlockSpec((1, dh), lambda b, h: (0, h)),
        ],
        out_specs=pl.BlockSpec((None, seq, dh), lambda b, h: (b, 0, h)),
        out_shape=jax.ShapeDtypeStruct((bsz, seq, width), BF16),
        scratch_shapes=[
            pltpu.VMEM((2, seq, dh), BF16),
            pltpu.VMEM((2, n_slab, 2 * slab, dh), BF16),
            pltpu.VMEM((2, seq, dh), BF16),
            pltpu.VMEM((2, seq, dh), BF16),
            pltpu.VMEM((seq, 2 * dh), BF16),
            pltpu.VMEM((2, n_slab * 8, dh), F32),
            pltpu.VMEM((n_slab, dh, 2 * dh), F32),
            pltpu.VMEM((n_slab, dh, 2 * dh), BF16),
        ],
        compiler_params=_params(("parallel", "parallel")),
        name="hgrn",
    )(rest, hf, hf, rest, rest, lb_fwd, lb_bwd, norm_g.reshape(1, width))


ATT_TQ = 128
ATT_HALF = 64
ATT_TK = ATT_TQ + 2 * ATT_HALF


def _attn_kernel(*refs, seq):
    n_g = len(ATTN_GROUPS)
    qkv_refs = refs[: 3 * n_g]
    cos_ref, sin_ref, out_ref = refs[3 * n_g: 3 * n_g + 3]
    qr_ref, kr_ref, vf_ref, qcm_ref, kcm_ref, vcm_ref, og_ref, lse_ref = refs[3 * n_g + 3:]
    dh = HEAD_DIM
    scale = HEAD_DIM ** -0.5

    lane = lax.broadcasted_iota(jnp.int32, (ATT_TK, dh), 1)

    def rope(x_ref, rows):
        t = x_ref[rows, :].astype(F32)
        swapped = jnp.where(lane < ROPE_DIM // 2,
                            pltpu.roll(t, dh - ROPE_DIM // 2, 1),
                            pltpu.roll(t, ROPE_DIM // 2, 1))
        return t * cos_ref[rows, :] + swapped * sin_ref[rows, :]

    qi_rel = lax.broadcasted_iota(jnp.int32, (ATT_TQ, ATT_TK), 0)
    kj_rel = lax.broadcasted_iota(jnp.int32, (ATT_TQ, ATT_TK), 1) - ATT_HALF
    band = jnp.abs(qi_rel - kj_rel) <= ATT_HALF

    for g, (window, dil) in enumerate(ATTN_GROUPS):
        assert window // (2 * dil) == ATT_HALF
        seg = seq // dil
        n_blk = seg // ATT_TQ
        assert n_blk & (n_blk - 1) == 0
        pitch = seg + ATT_TK
        q_ref, k_ref, v_ref = qkv_refs[3 * g: 3 * g + 3]

        def stage(s, carry, dil=dil, q_ref=q_ref, k_ref=k_ref, v_ref=v_ref):
            rows = pl.ds(pl.multiple_of(s * ATT_TK, ATT_TK), ATT_TK)
            q = rope(q_ref, rows)
            k = rope(k_ref, rows)
            if dil == 1:
                pad_rows = pl.ds(pl.multiple_of(s * ATT_TK + ATT_HALF, ATT_HALF), ATT_TK)
                qcm_ref[rows, :] = q.astype(BF16)
                kcm_ref[pad_rows, :] = k.astype(BF16)
                vcm_ref[pad_rows, :] = v_ref[rows, :]
            else:
                qr_ref[rows, :] = q
                kr_ref[rows, :] = k
                vf_ref[rows, :] = v_ref[rows, :].astype(F32)
            return carry

        lax.fori_loop(0, seq // ATT_TK, stage, 0, unroll=4)
        for cls in range(dil):
            base = cls * pitch
            for pad_ref, src_ref in ((kcm_ref, kr_ref), (vcm_ref, vf_ref)):
                pad_ref[pl.ds(base, ATT_HALF), :] = jnp.zeros((ATT_HALF, dh), BF16)
                pad_ref[pl.ds(base + ATT_HALF + seg, ATT_TK - ATT_HALF), :] = (
                    jnp.zeros((ATT_TK - ATT_HALF, dh), BF16))
                if dil > 1:
                    pad_ref[pl.ds(base + ATT_HALF, seg), :] = (
                        src_ref[pl.ds(cls, seg, stride=dil), :].astype(BF16))
            if dil > 1:
                qcm_ref[pl.ds(cls * seg, seg), :] = qr_ref[pl.ds(cls, seg, stride=dil), :].astype(BF16)

        def block(u, carry, g=g, dil=dil, seg=seg, n_blk=n_blk, pitch=pitch):
            cls = lax.shift_right_logical(u, n_blk.bit_length() - 1)
            q0 = (u & (n_blk - 1)) * ATT_TQ
            if dil == 1:
                tok_rows = pl.ds(pl.multiple_of(q0, ATT_TQ), ATT_TQ)
            else:
                tok_rows = pl.ds(cls + dil * q0, ATT_TQ, stride=dil)
            qb = qcm_ref[pl.ds(pl.multiple_of(u * ATT_TQ, ATT_TQ), ATT_TQ), :]
            win = pl.ds(pl.multiple_of(cls * pitch + q0, ATT_TQ), ATT_TK)
            s = _dot_nt(qb, kcm_ref[win, :]) * scale
            kj = kj_rel + q0
            valid = band & (kj >= 0) & (kj < seg)
            s = jnp.where(valid, s, NEG_INF)
            m = jnp.max(s, axis=-1, keepdims=True)
            p = jnp.exp(s - m)
            denom = jnp.sum(p, axis=-1, keepdims=True)
            o = _dot(p.astype(BF16), vcm_ref[win, :]) / denom
            lse = m + jnp.log(denom)
            og_ref[g, tok_rows, :] = o
            lse_ref[g, tok_rows, :] = jnp.broadcast_to(lse, (ATT_TQ, dh))
            return carry

        lax.fori_loop(0, dil * n_blk, block, 0, unroll=8)

    def combine(s, carry):
        rows = pl.ds(pl.multiple_of(s * ATT_TK, ATT_TK), ATT_TK)
        lses = [lse_ref[g, rows, :] for g in range(n_g)]
        mx = functools.reduce(jnp.maximum, lses)
        ws = [jnp.exp(l - mx) for l in lses]
        wsum = functools.reduce(lambda a, b: a + b, ws)
        acc = ws[0] * og_ref[0, rows, :]
        for g in range(1, n_g):
            acc = acc + ws[g] * og_ref[g, rows, :]
        out_ref[rows, :] = (acc / wsum).astype(out_ref.dtype)
        return carry

    lax.fori_loop(0, seq // ATT_TK, combine, 0)


def _rope_tables(seq):
    inv_freq = ROPE_THETA ** (-jnp.arange(0, ROPE_DIM, 2, dtype=F32) / ROPE_DIM)
    ang = jnp.arange(seq).astype(F32)[:, None] * inv_freq
    cos, sin = jnp.cos(ang), jnp.sin(ang)
    pad1 = jnp.ones((seq, HEAD_DIM - ROPE_DIM), F32)
    pad0 = jnp.zeros((seq, HEAD_DIM - ROPE_DIM), F32)
    return (jnp.concatenate([cos, cos, pad1], axis=1),
            jnp.concatenate([-sin, sin, pad0], axis=1))


def attn(rest, *, col_qkv):
    bsz, seq, _ = rest.shape
    dh = HEAD_DIM
    hpg = ATTN_HEADS_PER_GROUP
    n_g = len(ATTN_GROUPS)
    assert seq % (max(d for _, d in ATTN_GROUPS) * ATT_TQ) == 0
    cosf, sinf = _rope_tables(seq)

    def col(g, t):
        base = col_qkv // dh + (g * 3 + t) * hpg
        return lambda b, j: (b, 0, base + j)

    in_specs = [pl.BlockSpec((None, seq, dh), col(g, t)) for g in range(n_g) for t in range(3)]
    in_specs += [pl.BlockSpec((seq, dh), lambda b, j: (0, 0)) for _ in range(2)]
    pad_rows = max(dil * (seq // dil + ATT_TK) for _, dil in ATTN_GROUPS)
    return pl.pallas_call(
        functools.partial(_attn_kernel, seq=seq),
        grid=(bsz, hpg),
        in_specs=in_specs,
        out_specs=pl.BlockSpec((None, seq, dh), lambda b, j: (b, 0, j)),
        out_shape=jax.ShapeDtypeStruct((bsz, seq, hpg * dh), BF16),
        scratch_shapes=[
            pltpu.VMEM((seq, dh), F32), pltpu.VMEM((seq, dh), F32), pltpu.VMEM((seq, dh), F32),
            pltpu.VMEM((seq, dh), BF16),
            pltpu.VMEM((pad_rows, dh), BF16), pltpu.VMEM((pad_rows, dh), BF16),
            pltpu.VMEM((n_g, seq, dh), F32), pltpu.VMEM((n_g, seq, dh), F32),
        ],
        compiler_params=_params(("parallel", "parallel")),
        name="attn",
    )(*([rest] * (3 * n_g)), cosf, sinf)


def _mix_out_kernel(oa_ref, ob_ref, ga_ref, gb_ref, h_ref, wa_ref, wb_ref, wo_ref, g_ref, b_ref,
                    out_ref, *, alpha):
    ya = _dot(oa_ref[...], wa_ref[...])
    yb = _dot(ob_ref[...], wb_ref[...])
    z = _sigmoid(ga_ref[...].astype(F32)) * ya + _sigmoid(gb_ref[...].astype(F32)) * yb
    mix = _dot(z.astype(BF16), wo_ref[...])
    out_ref[...] = _layer_norm_rows(alpha * h_ref[...] + mix, g_ref[...], b_ref[...])


def mix_out(oa, ob, rest2d, h, wa_bf, wb_bf, wo_bf, g, b, *, alpha, col_gate, tm=256):
    m, d = h.shape
    wa_w = oa.shape[1]
    wb_w = ob.shape[1]
    assert m % tm == 0 and col_gate % d == 0
    const = lambda i: (0, 0)
    single = pl.Buffered(1)
    return pl.pallas_call(
        functools.partial(_mix_out_kernel, alpha=alpha),
        grid=(m // tm,),
        in_specs=[
            pl.BlockSpec((tm, wa_w), lambda i: (i, 0)),
            pl.BlockSpec((tm, wb_w), lambda i: (i, 0)),
            pl.BlockSpec((tm, d), lambda i: (i, col_gate // d)),
            pl.BlockSpec((tm, d), lambda i: (i, col_gate // d + 1)),
            pl.BlockSpec((tm, d), lambda i: (i, 0)),
            pl.BlockSpec((wa_w, d), const, pipeline_mode=single),
            pl.BlockSpec((wb_w, d), const, pipeline_mode=single),
            pl.BlockSpec((d, d), const, pipeline_mode=single),
            pl.BlockSpec((1, d), const),
            pl.BlockSpec((1, d), const),
        ],
        out_specs=pl.BlockSpec((tm, d), lambda i: (i, 0)),
        out_shape=jax.ShapeDtypeStruct((m, d), F32),
        compiler_params=_params(("parallel",)),
        name="mix_out",
    )(oa, ob, rest2d, rest2d, h, wa_bf, wb_bf, wo_bf, g.reshape(1, d), b.reshape(1, d))


def kernel(x, ffn1_w_in, ffn1_w_out, ln1_g, ln1_b, mix_w_in, hgrn_lb_fwd, hgrn_lb_bwd, hgrn_norm_g,
           w_branch_a, w_branch_b, mix_w_out, ln2_g, ln2_b, ffn2_w_in, ffn2_w_out, ln3_g, ln3_b):
    bsz, seq, d = x.shape
    depth = ffn1_w_in.shape[0]
    alpha = (2.0 * depth) ** 0.25
    hw = w_branch_a.shape[1]
    n_heads = hw // HGRN_HEAD_DIM
    qkv_w = len(ATTN_GROUPS) * 3 * ATTN_HEADS_PER_GROUP * HEAD_DIM
    m = bsz * seq

    h = x.reshape(m, d)
    for layer in range(depth):
        h, h_bf = ffn_ln(h, ffn1_w_in[layer].astype(BF16), ffn1_w_out[layer].astype(BF16),
                         ln1_g[layer], ln1_b[layer], alpha=alpha, emit_bf16=True)

        w_in = mix_w_in[layer]
        w_hf = w_in[:, hw:3 * hw].astype(BF16)
        w_rest = jnp.concatenate([w_in[:, :hw], w_in[:, 3 * hw:5 * hw], w_in[:, 5 * hw + qkv_w:],
                                  w_in[:, 5 * hw:5 * hw + qkv_w]], axis=1).astype(BF16)
        hf = matmul(h_bf, w_hf, F32).reshape(bsz, seq, 2 * hw)
        rest2d = matmul(h_bf, w_rest, BF16)
        rest = rest2d.reshape(bsz, seq, -1)
        col_hq, col_hi, col_hog, col_gate = 0, hw, 2 * hw, 3 * hw
        col_qkv = 3 * hw + 2 * d

        o_a = hgrn(rest, hf, hgrn_lb_fwd, hgrn_lb_bwd, hgrn_norm_g[layer], layer=layer,
                   n_heads=n_heads, col_hq=col_hq, col_hi=col_hi, col_hog=col_hog)
        o_b = attn(rest, col_qkv=col_qkv)

        h = mix_out(o_a.reshape(m, hw), o_b.reshape(m, -1), rest2d, h,
                    w_branch_a[layer].astype(BF16), w_branch_b[layer].astype(BF16),
                    mix_w_out[layer].astype(BF16), ln2_g[layer], ln2_b[layer],
                    alpha=alpha, col_gate=col_gate)

        (h,) = ffn_ln(h, ffn2_w_in[layer].astype(BF16), ffn2_w_out[layer].astype(BF16),
                      ln3_g[layer], ln3_b[layer], alpha=alpha, emit_bf16=False)
    return h.reshape(bsz, seq, d)
```

```python
import functools

import jax
import jax.numpy as jnp
import numpy as np
from jax import lax
from jax.experimental import pallas as pl
from jax.experimental.pallas import tpu as pltpu

F32 = jnp.float32
BF16 = jnp.bfloat16

HGRN_HEAD_DIM = 128
HGRN_CHUNK = 32
HGRN_SLAB = 4 * HGRN_CHUNK
ATTN_GROUPS = ((128, 1), (512, 4), (2048, 16))
ATTN_HEADS_PER_GROUP = 4
HEAD_DIM = 128
ROPE_THETA = 500000.0
ROPE_DIM = HEAD_DIM // 4
LN_EPS = 1e-5
NEG_INF = -1e30

V7X_LANES = 128
V7X_VMEM_LIMIT_BYTES = 60000 * 1024


def _params(semantics):
    return pltpu.CompilerParams(dimension_semantics=semantics,
                                vmem_limit_bytes=V7X_VMEM_LIMIT_BYTES)


def _dot(a, b):
    return jnp.dot(a, b, preferred_element_type=F32)


def _dot_nt(a, b):
    return lax.dot_general(a, b, (((1,), (1,)), ((), ())), preferred_element_type=F32)


def _dot_tn(a, b):
    return lax.dot_general(a, b, (((0,), (0,)), ((), ())), preferred_element_type=F32)


def _sigmoid(x):
    return 1.0 / (1.0 + jnp.exp(-x))


def _layer_norm_rows(y, g, b):
    mu = jnp.mean(y, axis=-1, keepdims=True)
    d = y - mu
    var = jnp.mean(d * d, axis=-1, keepdims=True)
    return d * lax.rsqrt(var + LN_EPS) * g + b


def _ffn_ln_kernel(x_ref, wg_ref, wu_ref, wo_ref, g_ref, b_ref, *rest, alpha, n_f):
    (out_ref, *maybe_outbf), (acc_ref, xbf_ref) = rest[:-2], rest[-2:]
    f = pl.program_id(1)

    @pl.when(f == 0)
    def _():
        acc_ref[...] = jnp.zeros_like(acc_ref)
        xbf_ref[...] = x_ref[...].astype(BF16)

    xb = xbf_ref[...]
    gate = _dot(xb, wg_ref[...])
    up = _dot(xb, wu_ref[...])
    hid = (gate * _sigmoid(gate)) * up
    acc_ref[...] += _dot(hid.astype(BF16), wo_ref[...])

    @pl.when(f == n_f - 1)
    def _():
        y = alpha * x_ref[...] + 0.5 * acc_ref[...]
        h = _layer_norm_rows(y, g_ref[...], b_ref[...])
        out_ref[...] = h
        for outbf_ref in maybe_outbf:
            outbf_ref[...] = h.astype(BF16)


def ffn_ln(x, w_in_bf, w_out_bf, g, b, *, alpha, emit_bf16, tm=1024, tf=512):
    m, d = x.shape
    ff = w_out_bf.shape[0]
    assert m % tm == 0 and ff % tf == 0
    n_f = ff // tf
    kern = functools.partial(_ffn_ln_kernel, alpha=alpha, n_f=n_f)
    out_dtypes = [F32, BF16] if emit_bf16 else [F32]
    once = pl.Buffered(1)
    return pl.pallas_call(
        kern,
        grid=(m // tm, n_f),
        in_specs=[
            pl.BlockSpec((tm, d), lambda i, f: (i, 0), pipeline_mode=once),
            pl.BlockSpec((d, tf), lambda i, f: (0, f)),
            pl.BlockSpec((d, tf), lambda i, f: (0, f + n_f)),
            pl.BlockSpec((tf, d), lambda i, f: (f, 0)),
            pl.BlockSpec((1, d), lambda i, f: (0, 0)),
            pl.BlockSpec((1, d), lambda i, f: (0, 0)),
        ],
        out_specs=[pl.BlockSpec((tm, d), lambda i, f: (i, 0), pipeline_mode=once) for _ in out_dtypes],
        out_shape=[jax.ShapeDtypeStruct((m, d), dt) for dt in out_dtypes],
        scratch_shapes=[pltpu.VMEM((tm, d), F32), pltpu.VMEM((tm, d), BF16)],
        compiler_params=_params(("parallel", "arbitrary")),
        name="ffn_ln",
    )(x, w_in_bf, w_in_bf, w_out_bf, g.reshape(1, d), b.reshape(1, d))


def _matmul_kernel(x_ref, w_ref, o_ref):
    o_ref[...] = _dot(x_ref[...], w_ref[...]).astype(o_ref.dtype)


def matmul(x_bf, w_bf, out_dtype, *, tm=2048, tn=512):
    m, k = x_bf.shape
    n = w_bf.shape[1]
    assert m % tm == 0 and n % tn == 0
    return pl.pallas_call(
        _matmul_kernel,
        grid=(m // tm, n // tn),
        in_specs=[pl.BlockSpec((tm, k), lambda i, j: (i, 0)),
                  pl.BlockSpec((k, tn), lambda i, j: (0, j))],
        out_specs=pl.BlockSpec((tm, tn), lambda i, j: (i, j)),
        out_shape=jax.ShapeDtypeStruct((m, n), out_dtype),
        compiler_params=_params(("parallel", "arbitrary")),
        name="in_proj",
    )(x_bf, w_bf)


def _split3(x):
    hi = x.astype(BF16)
    r1 = x - hi.astype(F32)
    mid = r1.astype(BF16)
    lo = (r1 - mid.astype(F32)).astype(BF16)
    return hi, mid, lo


def _hgrn_kernel(hq_ref, hff_ref, hfb_ref, hi_ref, hog_ref, lbf_ref, lbb_ref, ng_ref, out_ref,
                 qd_ref, kk_ref, q64_ref, k64_ref, q128_ref, dec_ref, kv_ref, snap_ref, *, seq, layer):
    c = HGRN_CHUNK
    slab = HGRN_SLAB
    n_slab = seq // slab
    dh = HGRN_HEAD_DIM
    assert slab == 4 * c

    def lower_bound(lb_ref):
        t = lb_ref[...].astype(F32)
        e = jnp.exp(t - jnp.max(t, axis=0, keepdims=True))
        sm = e / jnp.sum(e, axis=0, keepdims=True)
        return jnp.sum(sm[: layer + 1], axis=0, keepdims=True)

    lbs = (lower_bound(lbf_ref), lower_bound(lbb_ref))
    hf_refs = (hff_ref, hfb_ref)

    ri = lax.broadcasted_iota(jnp.int32, (slab, slab), 0)
    ci = lax.broadcasted_iota(jnp.int32, (slab, slab), 1)
    cr = ri // c
    cs = ci // c
    same = cr == cs
    m_diag = (same & (ci <= ri), same & (ci >= ri))
    m_adj = ((cr == cs + 1) & (cr % 2 == 1), (cr + 1 == cs) & (cr % 2 == 0))
    m_64 = ((cr >= 2) & (cs < 2), (cr < 2) & (cs >= 2))
    tmats = tuple(jnp.where(m, 1.0, 0.0).astype(BF16) for m in m_diag)

    def prep(s, carry):
        r0 = pl.multiple_of(s * slab, slab)
        rows = pl.ds(r0, slab)
        q = hq_ref[rows, :].astype(F32)
        q = q * _sigmoid(q)
        k128 = []
        for d in range(2):
            f = lbs[d] + (1.0 - lbs[d]) * _sigmoid(hf_refs[d][rows, :].astype(F32))
            k = 1.0 - f
            hi, mid, lo = _split3(jnp.log(f))
            cum = _dot(tmats[d], hi) + _dot(tmats[d], mid) + _dot(tmats[d], lo)
            tot_row = c - 1 if d == 0 else 0
            tot = [cum[j * c + tot_row: j * c + tot_row + 1, :] for j in range(4)]
            e = [jnp.exp(t) for t in tot]
            f0, f1 = e[0] * e[1], e[2] * e[3]
            if d == 0:
                qmul64, kmul64 = (None, e[0], None, e[2]), (e[1], None, e[3], None)
                qmul128, kmul128 = (None, None, f0, f0), (f1, f1, None, None)
            else:
                qmul64, kmul64 = (e[1], None, e[3], None), (None, e[0], None, e[2])
                qmul128, kmul128 = (f1, f1, None, None), (None, None, f0, f0)
            mul = lambda x, m: x if m is None else x * m
            pieces = []
            for j in range(4):
                sl = slice(j * c, (j + 1) * c)
                rows_j = pl.ds(r0 + j * c, c)
                cj, kj = cum[sl], k[sl]
                qd = q[sl] * jnp.exp(cj)
                ke = kj * jnp.exp(tot[j] - cj)
                q64 = mul(qd, qmul64[j])
                k64 = mul(ke, kmul64[j])
                qd_ref[d, rows_j, :] = qd.astype(BF16)
                kk_ref[d, s, pl.ds(j * c, c), :] = (kj * jnp.exp(-cj)).astype(BF16)
                kk_ref[d, s, pl.ds(slab + j * c, c), :] = ke.astype(BF16)
                q64_ref[d, rows_j, :] = q64.astype(BF16)
                k64_ref[d, rows_j, :] = k64.astype(BF16)
                q128_ref[rows_j, d * dh:(d + 1) * dh] = mul(q64, qmul128[j]).astype(BF16)
                pieces.append(mul(k64, kmul128[j]).astype(BF16))
            k128.append(jnp.concatenate(pieces, axis=0))
            dec_ref[d, pl.ds(pl.multiple_of(s * 8, 8), 8), :] = jnp.broadcast_to(f0 * f1, (8, dh))
        kv_ref[s] = _dot_tn(hi_ref[rows, :], jnp.concatenate(k128, axis=1))
        return carry

    lax.fori_loop(0, n_slab, prep, 0, unroll=4)

    def state_step(i, st, d):
        s = i if d == 0 else n_slab - 1 - i
        snap_ref[s, :, d * dh:(d + 1) * dh] = st.astype(BF16)
        return st * dec_ref[d, pl.ds(s * 8, 1), :] + kv_ref[s, :, d * dh:(d + 1) * dh]

    zero_state = jnp.zeros((dh, dh), F32)
    for d in range(2):
        lax.fori_loop(0, n_slab, functools.partial(state_step, d=d), zero_state)

    def out_slab(s, carry):
        rows = pl.ds(pl.multiple_of(s * slab, slab), slab)
        a = None
        for d in range(2):
            m1 = _dot_nt(qd_ref[d, rows, :], kk_ref[d, s])
            m2 = _dot_nt(q64_ref[d, rows, :], k64_ref[d, rows, :])
            ad = jnp.where(m_diag[d], m1[:, :slab],
                           jnp.where(m_adj[d], m1[:, slab:], jnp.where(m_64[d], m2, 0.0)))
            a = ad if a is None else a + ad
        o = _dot(a.astype(BF16), hi_ref[rows, :]) + _dot_nt(q128_ref[rows, :], snap_ref[s])
        o = o * lax.rsqrt(jnp.mean(o * o, axis=-1, keepdims=True) + LN_EPS)
        og = hog_ref[rows, :].astype(F32)
        out_ref[rows, :] = (o * ng_ref[...] * (og * _sigmoid(og))).astype(out_ref.dtype)
        return carry

    lax.fori_loop(0, n_slab, out_slab, 0, unroll=8)


def hgrn(rest, hf, lb_fwd, lb_bwd, norm_g, *, layer, n_heads, col_hq, col_hi, col_hog):
    bsz, seq, _ = rest.shape
    dh = HGRN_HEAD_DIM
    width = n_heads * dh
    nl = lb_fwd.shape[0]
    slab = HGRN_SLAB
    assert seq % slab == 0
    n_slab = seq // slab
    kern = functools.partial(_hgrn_kernel, seq=seq, layer=layer)

    def col(base):
        return lambda b, h: (b, 0, base // dh + h)

    return pl.pallas_call(
        kern,
        grid=(bsz, n_heads),
        in_specs=[
            pl.BlockSpec((None, seq, dh), col(col_hq)),
            pl.BlockSpec((None, seq, dh), col(0)),
            pl.BlockSpec((None, seq, dh), col(width)),
            pl.BlockSpec((None, seq, dh), col(col_hi)),
            pl.BlockSpec((None, seq, dh), col(col_hog)),
            pl.BlockSpec((nl, dh), lambda b, h: (0, h)),
            pl.BlockSpec((nl, dh), lambda b, h: (0, h)),
            pl.BlockSpec((1, dh), lambda b, h: (0, h)),
        ],
        out_specs=pl.BlockSpec((None, seq, dh), lambda b, h: (b, 0, h)),
        out_shape=jax.ShapeDtypeStruct((bsz, seq, width), BF16),
        scratch_shapes=[
            pltpu.VMEM((2, seq, dh), BF16),
            pltpu.VMEM((2, n_slab, 2 * slab, dh), BF16),
            pltpu.VMEM((2, seq, dh), BF16),
            pltpu.VMEM((2, seq, dh), BF16),
            pltpu.VMEM((seq, 2 * dh), BF16),
            pltpu.VMEM((2, n_slab * 8, dh), F32),
            pltpu.VMEM((n_slab, dh, 2 * dh), F32),
            pltpu.VMEM((n_slab, dh, 2 * dh), BF16),
        ],
        compiler_params=_params(("parallel", "parallel")),
        name="hgrn",
    )(rest, hf, hf, rest, rest, lb_fwd, lb_bwd, norm_g.reshape(1, width))


ATT_TQ = 128
ATT_HALF = 64
ATT_TK = ATT_TQ + 2 * ATT_HALF


def _attn_kernel(*refs, seq):
    n_g = len(ATTN_GROUPS)
    qkv_refs = refs[: 3 * n_g]
    cos_ref, sin_ref, out_ref = refs[3 * n_g: 3 * n_g + 3]
    qr_ref, kr_ref, vf_ref, qcm_ref, kcm_ref, vcm_ref, og_ref, lse_ref = refs[3 * n_g + 3:]
    dh = HEAD_DIM
    scale = HEAD_DIM ** -0.5

    lane = lax.broadcasted_iota(jnp.int32, (ATT_TK, dh), 1)

    def rope(x_ref, rows):
        t = x_ref[rows, :].astype(F32)
        swapped = jnp.where(lane < ROPE_DIM // 2,
                            pltpu.roll(t, dh - ROPE_DIM // 2, 1),
                            pltpu.roll(t, ROPE_DIM // 2, 1))
        return t * cos_ref[rows, :] + swapped * sin_ref[rows, :]

    qi_rel = lax.broadcasted_iota(jnp.int32, (ATT_TQ, ATT_TK), 0)
    kj_rel = lax.broadcasted_iota(jnp.int32, (ATT_TQ, ATT_TK), 1) - ATT_HALF
    band = jnp.abs(qi_rel - kj_rel) <= ATT_HALF

    for g, (window, dil) in enumerate(ATTN_GROUPS):
        assert window // (2 * dil) == ATT_HALF
        seg = seq // dil
        n_blk = seg // ATT_TQ
        assert n_blk & (n_blk - 1) == 0
        pitch = seg + ATT_TK
        q_ref, k_ref, v_ref = qkv_refs[3 * g: 3 * g + 3]

        def stage(s, carry, dil=dil, q_ref=q_ref, k_ref=k_ref, v_ref=v_ref):
            rows = pl.ds(pl.multiple_of(s * ATT_TK, ATT_TK), ATT_TK)
            q = rope(q_ref, rows)
            k = rope(k_ref, rows)
            if dil == 1:
                pad_rows = pl.ds(pl.multiple_of(s * ATT_TK + ATT_HALF, ATT_HALF), ATT_TK)
                qcm_ref[rows, :] = q.astype(BF16)
                kcm_ref[pad_rows, :] = k.astype(BF16)
                vcm_ref[pad_rows, :] = v_ref[rows, :]
            else:
                qr_ref[rows, :] = q
                kr_ref[rows, :] = k
                vf_ref[rows, :] = v_ref[rows, :].astype(F32)
            return carry

        lax.fori_loop(0, seq // ATT_TK, stage, 0, unroll=4)
        for cls in range(dil):
            base = cls * pitch
            for pad_ref, src_ref in ((kcm_ref, kr_ref), (vcm_ref, vf_ref)):
                pad_ref[pl.ds(base, ATT_HALF), :] = jnp.zeros((ATT_HALF, dh), BF16)
                pad_ref[pl.ds(base + ATT_HALF + seg, ATT_TK - ATT_HALF), :] = (
                    jnp.zeros((ATT_TK - ATT_HALF, dh), BF16))
                if dil > 1:
                    pad_ref[pl.ds(base + ATT_HALF, seg), :] = (
                        src_ref[pl.ds(cls, seg, stride=dil), :].astype(BF16))
            if dil > 1:
                qcm_ref[pl.ds(cls * seg, seg), :] = qr_ref[pl.ds(cls, seg, stride=dil), :].astype(BF16)

        def block(u, carry, g=g, dil=dil, seg=seg, n_blk=n_blk, pitch=pitch):
            cls = lax.shift_right_logical(u, n_blk.bit_length() - 1)
            q0 = (u & (n_blk - 1)) * ATT_TQ
            if dil == 1:
                tok_rows = pl.ds(pl.multiple_of(q0, ATT_TQ), ATT_TQ)
            else:
                tok_rows = pl.ds(cls + dil * q0, ATT_TQ, stride=dil)
            qb = qcm_ref[pl.ds(pl.multiple_of(u * ATT_TQ, ATT_TQ), ATT_TQ), :]
            win = pl.ds(pl.multiple_of(cls * pitch + q0, ATT_TQ), ATT_TK)
            s = _dot_nt(qb, kcm_ref[win, :]) * scale
            kj = kj_rel + q0
            valid = band & (kj >= 0) & (kj < seg)
            s = jnp.where(valid, s, NEG_INF)
            m = jnp.max(s, axis=-1, keepdims=True)
            p = jnp.exp(s - m)
            denom = jnp.sum(p, axis=-1, keepdims=True)
            o = _dot(p.astype(BF16), vcm_ref[win, :]) / denom
            lse = m + jnp.log(denom)
            og_ref[g, tok_rows, :] = o
            lse_ref[g, tok_rows, :] = jnp.broadcast_to(lse, (ATT_TQ, dh))
            return carry

        lax.fori_loop(0, dil * n_blk, block, 0, unroll=8)

    def combine(s, carry):
        rows = pl.ds(pl.multiple_of(s * ATT_TK, ATT_TK), ATT_TK)
        lses = [lse_ref[g, rows, :] for g in range(n_g)]
        mx = functools.reduce(jnp.maximum, lses)
        ws = [jnp.exp(l - mx) for l in lses]
        wsum = functools.reduce(lambda a, b: a + b, ws)
        acc = ws[0] * og_ref[0, rows, :]
        for g in range(1, n_g):
            acc = acc + ws[g] * og_ref[g, rows, :]
        out_ref[rows, :] = (acc / wsum).astype(out_ref.dtype)
        return carry

    lax.fori_loop(0, seq // ATT_TK, combine, 0)


def _rope_tables(seq):
    inv_freq = ROPE_THETA ** (-jnp.arange(0, ROPE_DIM, 2, dtype=F32) / ROPE_DIM)
    ang = jnp.arange(seq).astype(F32)[:, None] * inv_freq
    cos, sin = jnp.cos(ang), jnp.sin(ang)
    pad1 = jnp.ones((seq, HEAD_DIM - ROPE_DIM), F32)
    pad0 = jnp.zeros((seq, HEAD_DIM - ROPE_DIM), F32)
    return (jnp.concatenate([cos, cos, pad1], axis=1),
            jnp.concatenate([-sin, sin, pad0], axis=1))


def attn(rest, *, col_qkv):
    bsz, seq, _ = rest.shape
    dh = HEAD_DIM
    hpg = ATTN_HEADS_PER_GROUP
    n_g = len(ATTN_GROUPS)
    assert seq % (max(d for _, d in ATTN_GROUPS) * ATT_TQ) == 0
    cosf, sinf = _rope_tables(seq)

    def col(g, t):
        base = col_qkv // dh + (g * 3 + t) * hpg
        return lambda b, j: (b, 0, base + j)

    in_specs = [pl.BlockSpec((None, seq, dh), col(g, t)) for g in range(n_g) for t in range(3)]
    in_specs += [pl.BlockSpec((seq, dh), lambda b, j: (0, 0)) for _ in range(2)]
    pad_rows = max(dil * (seq // dil + ATT_TK) for _, dil in ATTN_GROUPS)
    return pl.pallas_call(
        functools.partial(_attn_kernel, seq=seq),
        grid=(bsz, hpg),
        in_specs=in_specs,
        out_specs=pl.BlockSpec((None, seq, dh), lambda b, j: (b, 0, j)),
        out_shape=jax.ShapeDtypeStruct((bsz, seq, hpg * dh), BF16),
        scratch_shapes=[
            pltpu.VMEM((seq, dh), F32), pltpu.VMEM((seq, dh), F32), pltpu.VMEM((seq, dh), F32),
            pltpu.VMEM((seq, dh), BF16),
            pltpu.VMEM((pad_rows, dh), BF16), pltpu.VMEM((pad_rows, dh), BF16),
            pltpu.VMEM((n_g, seq, dh), F32), pltpu.VMEM((n_g, seq, dh), F32),
        ],
        compiler_params=_params(("parallel", "parallel")),
        name="attn",
    )(*([rest] * (3 * n_g)), cosf, sinf)


def _mix_out_kernel(oa_ref, ob_ref, ga_ref, gb_ref, h_ref, wa_ref, wb_ref, wo_ref, g_ref, b_ref,
                    out_ref, *, alpha):
    ya = _dot(oa_ref[...], wa_ref[...])
    yb = _dot(ob_ref[...], wb_ref[...])
    z = _sigmoid(ga_ref[...].astype(F32)) * ya + _sigmoid(gb_ref[...].astype(F32)) * yb
    mix = _dot(z.astype(BF16), wo_ref[...])
    out_ref[...] = _layer_norm_rows(alpha * h_ref[...] + mix, g_ref[...], b_ref[...])


def mix_out(oa, ob, rest2d, h, wa_bf, wb_bf, wo_bf, g, b, *, alpha, col_gate, tm=256):
    m, d = h.shape
    wa_w = oa.shape[1]
    wb_w = ob.shape[1]
    assert m % tm == 0 and col_gate % d == 0
    const = lambda i: (0, 0)
    single = pl.Buffered(1)
    return pl.pallas_call(
        functools.partial(_mix_out_kernel, alpha=alpha),
        grid=(m // tm,),
        in_specs=[
            pl.BlockSpec((tm, wa_w), lambda i: (i, 0)),
            pl.BlockSpec((tm, wb_w), lambda i: (i, 0)),
            pl.BlockSpec((tm, d), lambda i: (i, col_gate // d)),
            pl.BlockSpec((tm, d), lambda i: (i, col_gate // d + 1)),
            pl.BlockSpec((tm, d), lambda i: (i, 0)),
            pl.BlockSpec((wa_w, d), const, pipeline_mode=single),
            pl.BlockSpec((wb_w, d), const, pipeline_mode=single),
            pl.BlockSpec((d, d), const, pipeline_mode=single),
            pl.BlockSpec((1, d), const),
            pl.BlockSpec((1, d), const),
        ],
        out_specs=pl.BlockSpec((tm, d), lambda i: (i, 0)),
        out_shape=jax.ShapeDtypeStruct((m, d), F32),
        compiler_params=_params(("parallel",)),
        name="mix_out",
    )(oa, ob, rest2d, rest2d, h, wa_bf, wb_bf, wo_bf, g.reshape(1, d), b.reshape(1, d))


def kernel(x, ffn1_w_in, ffn1_w_out, ln1_g, ln1_b, mix_w_in, hgrn_lb_fwd, hgrn_lb_bwd, hgrn_norm_g,
           w_branch_a, w_branch_b, mix_w_out, ln2_g, ln2_b, ffn2_w_in, ffn2_w_out, ln3_g, ln3_b):
    bsz, seq, d = x.shape
    depth = ffn1_w_in.shape[0]
    alpha = (2.0 * depth) ** 0.25
    hw = w_branch_a.shape[1]
    n_heads = hw // HGRN_HEAD_DIM
    qkv_w = len(ATTN_GROUPS) * 3 * ATTN_HEADS_PER_GROUP * HEAD_DIM
    m = bsz * seq

    h = x.reshape(m, d)
    for layer in range(depth):
        h, h_bf = ffn_ln(h, ffn1_w_in[layer].astype(BF16), ffn1_w_out[layer].astype(BF16),
                         ln1_g[layer], ln1_b[layer], alpha=alpha, emit_bf16=True)

        w_in = mix_w_in[layer]
        w_hf = w_in[:, hw:3 * hw].astype(BF16)
        w_rest = jnp.concatenate([w_in[:, :hw], w_in[:, 3 * hw:5 * hw], w_in[:, 5 * hw + qkv_w:],
                                  w_in[:, 5 * hw:5 * hw + qkv_w]], axis=1).astype(BF16)
        hf = matmul(h_bf, w_hf, F32).reshape(bsz, seq, 2 * hw)
        rest2d = matmul(h_bf, w_rest, BF16)
        rest = rest2d.reshape(bsz, seq, -1)
        col_hq, col_hi, col_hog, col_gate = 0, hw, 2 * hw, 3 * hw
        col_qkv = 3 * hw + 2 * d

        o_a = hgrn(rest, hf, hgrn_lb_fwd, hgrn_lb_bwd, hgrn_norm_g[layer], layer=layer,
                   n_heads=n_heads, col_hq=col_hq, col_hi=col_hi, col_hog=col_hog)
        o_b = attn(rest, col_qkv=col_qkv)

        h = mix_out(o_a.reshape(m, hw), o_b.reshape(m, -1), rest2d, h,
                    w_branch_a[layer].astype(BF16), w_branch_b[layer].astype(BF16),
                    mix_w_out[layer].astype(BF16), ln2_g[layer], ln2_b[layer],
                    alpha=alpha, col_gate=col_gate)

        (h,) = ffn_ln(h, ffn2_w_in[layer].astype(BF16), ffn2_w_out[layer].astype(BF16),
                      ln3_g[layer], ln3_b[layer], alpha=alpha, emit_bf16=False)
    return h.reshape(bsz, seq, d)
```

```python
import functools

import jax
import jax.numpy as jnp
from jax import lax
from jax.experimental import pallas as pl
from jax.experimental.pallas import tpu as pltpu

F32 = jnp.float32
BF16 = jnp.bfloat16

HGRN_HEAD_DIM = 128
HGRN_CHUNK = 32
HGRN_SLAB = 4 * HGRN_CHUNK
ATTN_GROUPS = ((128, 1), (512, 4), (2048, 16))
ATTN_HEADS_PER_GROUP = 4
HEAD_DIM = 128
ROPE_THETA = 500000.0
ROPE_DIM = HEAD_DIM // 4
LN_EPS = 1e-5
NEG_INF = -1e30

V7X_VMEM_LIMIT_BYTES = 60000 * 1024


def _params(semantics):
    return pltpu.CompilerParams(dimension_semantics=semantics,
                                vmem_limit_bytes=V7X_VMEM_LIMIT_BYTES)


def _dot(a, b):
    return jnp.dot(a, b, preferred_element_type=F32)


def _dot_nt(a, b):
    return lax.dot_general(a, b, (((1,), (1,)), ((), ())), preferred_element_type=F32)


def _dot_tn(a, b):
    return lax.dot_general(a, b, (((0,), (0,)), ((), ())), preferred_element_type=F32)


def _sigmoid(x):
    return 1.0 / (1.0 + jnp.exp(-x))


def _layer_norm_rows(y, g, b):
    mu = jnp.mean(y, axis=-1, keepdims=True)
    d = y - mu
    var = jnp.mean(d * d, axis=-1, keepdims=True)
    return d * lax.rsqrt(var + LN_EPS) * g + b


def _ffn_ln_kernel(x_ref, wg_ref, wu_ref, wo_ref, g_ref, b_ref, *rest, alpha, n_f):
    (out_ref, *maybe_outbf), (acc_ref, xbf_ref) = rest[:-2], rest[-2:]
    f = pl.program_id(1)

    @pl.when(f == 0)
    def _():
        acc_ref[...] = jnp.zeros_like(acc_ref)
        xbf_ref[...] = x_ref[...].astype(BF16)

    xb = xbf_ref[...]
    gate = _dot(xb, wg_ref[...])
    up = _dot(xb, wu_ref[...])
    hid = (gate * _sigmoid(gate)) * up
    acc_ref[...] += _dot(hid.astype(BF16), wo_ref[...])

    @pl.when(f == n_f - 1)
    def _():
        y = alpha * x_ref[...] + 0.5 * acc_ref[...]
        h = _layer_norm_rows(y, g_ref[...], b_ref[...])
        out_ref[...] = h
        for outbf_ref in maybe_outbf:
            outbf_ref[...] = h.astype(BF16)


def ffn_ln(x, w_in_bf, w_out_bf, g, b, *, alpha, emit_bf16, tm=512, tf=512):
    m, d = x.shape
    ff = w_out_bf.shape[0]
    assert m % tm == 0 and ff % tf == 0
    n_f = ff // tf
    kern = functools.partial(_ffn_ln_kernel, alpha=alpha, n_f=n_f)
    out_dtypes = [F32, BF16] if emit_bf16 else [F32]
    return pl.pallas_call(
        kern,
        grid=(m // tm, n_f),
        in_specs=[
            pl.BlockSpec((tm, d), lambda i, f: (i, 0)),
            pl.BlockSpec((d, tf), lambda i, f: (0, f)),
            pl.BlockSpec((d, tf), lambda i, f: (0, f + n_f)),
            pl.BlockSpec((tf, d), lambda i, f: (f, 0)),
            pl.BlockSpec((1, d), lambda i, f: (0, 0)),
            pl.BlockSpec((1, d), lambda i, f: (0, 0)),
        ],
        out_specs=[pl.BlockSpec((tm, d), lambda i, f: (i, 0)) for _ in out_dtypes],
        out_shape=[jax.ShapeDtypeStruct((m, d), dt) for dt in out_dtypes],
        scratch_shapes=[pltpu.VMEM((tm, d), F32), pltpu.VMEM((tm, d), BF16)],
        compiler_params=_params(("parallel", "arbitrary")),
        name="ffn_ln",
    )(x, w_in_bf, w_in_bf, w_out_bf, g.reshape(1, d), b.reshape(1, d))


def _matmul_kernel(x_ref, w_ref, o_ref):
    o_ref[...] = _dot(x_ref[...], w_ref[...]).astype(o_ref.dtype)


def matmul(x_bf, w_bf, out_dtype, *, tm=2048, tn=512):
    m, k = x_bf.shape
    n = w_bf.shape[1]
    assert m % tm == 0 and n % tn == 0
    return pl.pallas_call(
        _matmul_kernel,
        grid=(m // tm, n // tn),
        in_specs=[pl.BlockSpec((tm, k), lambda i, j: (i, 0)),
                  pl.BlockSpec((k, tn), lambda i, j: (0, j))],
        out_specs=pl.BlockSpec((tm, tn), lambda i, j: (i, j)),
        out_shape=jax.ShapeDtypeStruct((m, n), out_dtype),
        compiler_params=_params(("parallel", "arbitrary")),
        name="in_proj",
    )(x_bf, w_bf)


def _split2(x):
    hi = x.astype(BF16)
    lo = (x - hi.astype(F32)).astype(BF16)
    return hi, lo


def _hgrn_kernel(hq_ref, hff_ref, hfb_ref, hi_ref, hog_ref, lbf_ref, lbb_ref, ng_ref, out_ref,
                 qd_ref, qmix_ref, q64_ref, q128_ref, kdt_ref, kmixt_ref, k64t_ref,
                 dec_ref, kv_ref, snap_ref, *, seq, layer):
    c = HGRN_CHUNK
    slab = HGRN_SLAB
    n_slab = seq // slab
    dh = HGRN_HEAD_DIM
    assert slab == 4 * c

    def lower_bound(lb_ref):
        t = lb_ref[...].astype(F32)
        e = jnp.exp(t - jnp.max(t, axis=0, keepdims=True))
        sm = e / jnp.sum(e, axis=0, keepdims=True)
        return jnp.sum(sm[: layer + 1], axis=0, keepdims=True)

    lbs = (lower_bound(lbf_ref), lower_bound(lbb_ref))
    hf_refs = (hff_ref, hfb_ref)

    ri = lax.broadcasted_iota(jnp.int32, (slab, slab), 0)
    ci = lax.broadcasted_iota(jnp.int32, (slab, slab), 1)
    same_chunk = (ri // c) == (ci // c)
    same_pair = (ri // (2 * c)) == (ci // (2 * c))
    lower = ci <= ri
    upper = ci >= ri
    tmats = tuple(jnp.where(same_chunk & m, 1.0, 0.0).astype(BF16) for m in (lower, upper))

    def prep(s, carry):
        r0 = pl.multiple_of(s * slab, slab)
        rows = pl.ds(r0, slab)
        q = hq_ref[rows, :].astype(F32)
        q = q * _sigmoid(q)
        qd, ke, e = [], [], []
        for d in range(2):
            f = lbs[d] + (1.0 - lbs[d]) * _sigmoid(hf_refs[d][rows, :].astype(F32))
            hi, lo = _split2(jnp.log(f))
            cum = _dot(tmats[d], hi) + _dot(tmats[d], lo)
            tot_row = c - 1 if d == 0 else 0
            e.append([jnp.exp(cum[j * c + tot_row: j * c + tot_row + 1, :]) for j in range(4)])
            ecum = jnp.exp(cum)
            qd_d = q * ecum
            kd_d = (1.0 - f) / ecum
            qd_ref[d, rows, :] = qd_d.astype(BF16)
            kdt_ref[d, s] = kd_d.T.astype(BF16)
            qd.append([qd_d[j * c:(j + 1) * c] for j in range(4)])
            ke.append([kd_d[j * c:(j + 1) * c] * e[d][j] for j in range(4)])

        cat = lambda pieces: jnp.concatenate(pieces, axis=0)
        qmix_ref[rows, :] = cat([qd[1][0], qd[0][1], qd[1][2], qd[0][3]]).astype(BF16)
        kmixt_ref[s] = cat([ke[0][0], ke[1][1], ke[0][2], ke[1][3]]).T.astype(BF16)
        q64_ref[rows, :] = cat([qd[1][0] * e[1][1], qd[1][1], qd[0][2], qd[0][3] * e[0][2]]).astype(BF16)
        k64t_ref[s] = cat([ke[0][0] * e[0][1], ke[0][1], ke[1][2], ke[1][3] * e[1][2]]).T.astype(BF16)
        k128 = []
        for d in range(2):
            e0, e1, e2, e3 = e[d]
            from_start = (None, e0, e0 * e1, e0 * e1 * e2)
            to_end = (e1 * e2 * e3, e2 * e3, e3, None)
            qmul, kmul = (from_start, to_end) if d == 0 else (to_end, from_start)
            mul = lambda x, m: x if m is None else x * m
            q128_ref[rows, d * dh:(d + 1) * dh] = cat([mul(qd[d][j], qmul[j]) for j in range(4)]).astype(BF16)
            k128.append(cat([mul(ke[d][j], kmul[j]) for j in range(4)]).astype(BF16))
            dec_ref[d, pl.ds(pl.multiple_of(s * 8, 8), 8), :] = jnp.broadcast_to(e0 * e1 * e2 * e3, (8, dh))
        kv_ref[s] = _dot_tn(hi_ref[rows, :], jnp.concatenate(k128, axis=1))
        return carry

    lax.fori_loop(0, n_slab, prep, 0, unroll=4)

    def state_step(i, st, d):
        s = i if d == 0 else n_slab - 1 - i
        snap_ref[s, :, d * dh:(d + 1) * dh] = st
        return st * dec_ref[d, pl.ds(s * 8, 1), :] + kv_ref[s, :, d * dh:(d + 1) * dh]

    zero_state = jnp.zeros((dh, dh), F32)
    for d in range(2):
        lax.fori_loop(0, n_slab, functools.partial(state_step, d=d), zero_state)

    def out_slab(s, carry):
        rows = pl.ds(pl.multiple_of(s * slab, slab), slab)
        m_df = _dot(qd_ref[0, rows, :], kdt_ref[0, s])
        m_db = _dot(qd_ref[1, rows, :], kdt_ref[1, s])
        m_32 = _dot(qmix_ref[rows, :], kmixt_ref[s])
        m_64 = _dot(q64_ref[rows, :], k64t_ref[s])
        a = jnp.where(same_chunk,
                      jnp.where(lower, m_df, 0.0) + jnp.where(upper, m_db, 0.0),
                      jnp.where(same_pair, m_32, m_64))
        states = jnp.concatenate([snap_ref[s, :, :dh].T, snap_ref[s, :, dh:].T], axis=0)
        o = _dot(a.astype(BF16), hi_ref[rows, :]) + _dot(q128_ref[rows, :], states.astype(BF16))
        o = o * lax.rsqrt(jnp.mean(o * o, axis=-1, keepdims=True) + LN_EPS)
        og = hog_ref[rows, :].astype(F32)
        out_ref[rows, :] = (o * ng_ref[...] * (og * _sigmoid(og))).astype(out_ref.dtype)
        return carry

    lax.fori_loop(0, n_slab, out_slab, 0, unroll=8)


def hgrn(rest, hf, lb_fwd, lb_bwd, norm_g, *, layer, n_heads, col_hq, col_hi, col_hog):
    bsz, seq, _ = rest.shape
    dh = HGRN_HEAD_DIM
    width = n_heads * dh
    nl = lb_fwd.shape[0]
    slab = HGRN_SLAB
    assert seq % slab == 0
    n_slab = seq // slab
    kern = functools.partial(_hgrn_kernel, seq=seq, layer=layer)

    def col(base):
        return lambda b, h: (b, 0, base // dh + h)

    return pl.pallas_call(
        kern,
        grid=(bsz, n_heads),
        in_specs=[
            pl.BlockSpec((None, seq, dh), col(col_hq)),
            pl.BlockSpec((None, seq, dh), col(0)),
            pl.BlockSpec((None, seq, dh), col(width)),
            pl.BlockSpec((None, seq, dh), col(col_hi)),
            pl.BlockSpec((None, seq, dh), col(col_hog)),
            pl.BlockSpec((nl, dh), lambda b, h: (0, h)),
            pl.BlockSpec((nl, dh), lambda b, h: (0, h)),
            pl.BlockSpec((1, dh), lambda b, h: (0, h)),
        ],
        out_specs=pl.BlockSpec((None, seq, dh), lambda b, h: (b, 0, h)),
        out_shape=jax.ShapeDtypeStruct((bsz, seq, width), BF16),
        scratch_shapes=[
            pltpu.VMEM((2, seq, dh), BF16),
            pltpu.VMEM((seq, dh), BF16),
            pltpu.VMEM((seq, dh), BF16),
            pltpu.VMEM((seq, 2 * dh), BF16),
            pltpu.VMEM((2, n_slab, dh, slab), BF16),
            pltpu.VMEM((n_slab, dh, slab), BF16),
            pltpu.VMEM((n_slab, dh, slab), BF16),
            pltpu.VMEM((2, n_slab * 8, dh), F32),
            pltpu.VMEM((n_slab, dh, 2 * dh), F32),
            pltpu.VMEM((n_slab, dh, 2 * dh), F32),
        ],
        compiler_params=_params(("parallel", "parallel")),
        name="hgrn",
    )(rest, hf, hf, rest, rest, lb_fwd, lb_bwd, norm_g.reshape(1, width))


ATT_TQ = 128
ATT_HALF = 64
ATT_TK = ATT_TQ + 2 * ATT_HALF


def _attn_kernel(*refs, seq):
    n_g = len(ATTN_GROUPS)
    qkv_refs = refs[: 3 * n_g]
    cos_ref, sin_ref, out_ref = refs[3 * n_g: 3 * n_g + 3]
    qr_ref, kr_ref, vf_ref, qcm_ref, kcm_ref, vcm_ref, og_ref, lse_ref = refs[3 * n_g + 3:]
    dh = HEAD_DIM
    scale = HEAD_DIM ** -0.5

    lane = lax.broadcasted_iota(jnp.int32, (ATT_TK, dh), 1)

    def rope(x_ref, rows):
        t = x_ref[rows, :].astype(F32)
        swapped = jnp.where(lane < ROPE_DIM // 2,
                            pltpu.roll(t, dh - ROPE_DIM // 2, 1),
                            pltpu.roll(t, ROPE_DIM // 2, 1))
        return t * cos_ref[rows, :] + swapped * sin_ref[rows, :]

    qi_rel = lax.broadcasted_iota(jnp.int32, (ATT_TQ, ATT_TK), 0)
    kj_rel = lax.broadcasted_iota(jnp.int32, (ATT_TQ, ATT_TK), 1) - ATT_HALF
    band = jnp.abs(qi_rel - kj_rel) <= ATT_HALF

    for g, (window, dil) in enumerate(ATTN_GROUPS):
        assert window // (2 * dil) == ATT_HALF
        seg = seq // dil
        n_blk = seg // ATT_TQ
        assert n_blk & (n_blk - 1) == 0
        pitch = seg + ATT_TK
        q_ref, k_ref, v_ref = qkv_refs[3 * g: 3 * g + 3]

        def stage(s, carry, dil=dil, q_ref=q_ref, k_ref=k_ref, v_ref=v_ref):
            rows = pl.ds(pl.multiple_of(s * ATT_TK, ATT_TK), ATT_TK)
            q = rope(q_ref, rows)
            k = rope(k_ref, rows)
            if dil == 1:
                pad_rows = pl.ds(pl.multiple_of(s * ATT_TK + ATT_HALF, ATT_HALF), ATT_TK)
                qcm_ref[rows, :] = q.astype(BF16)
                kcm_ref[pad_rows, :] = k.astype(BF16)
                vcm_ref[pad_rows, :] = v_ref[rows, :]
            else:
                qr_ref[rows, :] = q
                kr_ref[rows, :] = k
                vf_ref[rows, :] = v_ref[rows, :].astype(F32)
            return carry

        lax.fori_loop(0, seq // ATT_TK, stage, 0, unroll=4)
        for cls in range(dil):
            base = cls * pitch
            for pad_ref, src_ref in ((kcm_ref, kr_ref), (vcm_ref, vf_ref)):
                pad_ref[pl.ds(base, ATT_HALF), :] = jnp.zeros((ATT_HALF, dh), BF16)
                pad_ref[pl.ds(base + ATT_HALF + seg, ATT_TK - ATT_HALF), :] = (
                    jnp.zeros((ATT_TK - ATT_HALF, dh), BF16))
                if dil > 1:
                    pad_ref[pl.ds(base + ATT_HALF, seg), :] = (
                        src_ref[pl.ds(cls, seg, stride=dil), :].astype(BF16))
            if dil > 1:
                qcm_ref[pl.ds(cls * seg, seg), :] = qr_ref[pl.ds(cls, seg, stride=dil), :].astype(BF16)

        def block(u, carry, g=g, dil=dil, seg=seg, n_blk=n_blk, pitch=pitch):
            cls = lax.shift_right_logical(u, n_blk.bit_length() - 1)
            q0 = (u & (n_blk - 1)) * ATT_TQ
            if dil == 1:
                tok_rows = pl.ds(pl.multiple_of(q0, ATT_TQ), ATT_TQ)
            else:
                tok_rows = pl.ds(cls + dil * q0, ATT_TQ, stride=dil)
            qb = qcm_ref[pl.ds(pl.multiple_of(u * ATT_TQ, ATT_TQ), ATT_TQ), :]
            win = pl.ds(pl.multiple_of(cls * pitch + q0, ATT_TQ), ATT_TK)
            s = _dot_nt(qb, kcm_ref[win, :]) * scale
            kj = kj_rel + q0
            valid = band & (kj >= 0) & (kj < seg)
            s = jnp.where(valid, s, NEG_INF)
            m = jnp.max(s, axis=-1, keepdims=True)
            p = jnp.exp(s - m)
            denom = jnp.sum(p, axis=-1, keepdims=True)
            o = _dot(p.astype(BF16), vcm_ref[win, :]) / denom
            lse = m + jnp.log(denom)
            og_ref[g, tok_rows, :] = o
            lse_ref[g, tok_rows, :] = jnp.broadcast_to(lse, (ATT_TQ, dh))
            return carry

        lax.fori_loop(0, dil * n_blk, block, 0, unroll=8)

    def combine(s, carry):
        rows = pl.ds(pl.multiple_of(s * ATT_TK, ATT_TK), ATT_TK)
        lses = [lse_ref[g, rows, :] for g in range(n_g)]
        mx = functools.reduce(jnp.maximum, lses)
        ws = [jnp.exp(l - mx) for l in lses]
        wsum = functools.reduce(lambda a, b: a + b, ws)
        acc = ws[0] * og_ref[0, rows, :]
        for g in range(1, n_g):
            acc = acc + ws[g] * og_ref[g, rows, :]
        out_ref[rows, :] = (acc / wsum).astype(out_ref.dtype)
        return carry

    lax.fori_loop(0, seq // ATT_TK, combine, 0)


def _rope_tables(seq):
    inv_freq = ROPE_THETA ** (-jnp.arange(0, ROPE_DIM, 2, dtype=F32) / ROPE_DIM)
    ang = jnp.arange(seq).astype(F32)[:, None] * inv_freq
    cos, sin = jnp.cos(ang), jnp.sin(ang)
    pad1 = jnp.ones((seq, HEAD_DIM - ROPE_DIM), F32)
    pad0 = jnp.zeros((seq, HEAD_DIM - ROPE_DIM), F32)
    return (jnp.concatenate([cos, cos, pad1], axis=1),
            jnp.concatenate([-sin, sin, pad0], axis=1))


def attn(rest, *, col_qkv):
    bsz, seq, _ = rest.shape
    dh = HEAD_DIM
    hpg = ATTN_HEADS_PER_GROUP
    n_g = len(ATTN_GROUPS)
    assert seq % (max(d for _, d in ATTN_GROUPS) * ATT_TQ) == 0
    cosf, sinf = _rope_tables(seq)

    def col(g, t):
        base = col_qkv // dh + (g * 3 + t) * hpg
        return lambda b, j: (b, 0, base + j)

    in_specs = [pl.BlockSpec((None, seq, dh), col(g, t)) for g in range(n_g) for t in range(3)]
    in_specs += [pl.BlockSpec((seq, dh), lambda b, j: (0, 0)) for _ in range(2)]
    pad_rows = max(dil * (seq // dil + ATT_TK) for _, dil in ATTN_GROUPS)
    return pl.pallas_call(
        functools.partial(_attn_kernel, seq=seq),
        grid=(bsz, hpg),
        in_specs=in_specs,
        out_specs=pl.BlockSpec((None, seq, dh), lambda b, j: (b, 0, j)),
        out_shape=jax.ShapeDtypeStruct((bsz, seq, hpg * dh), BF16),
        scratch_shapes=[
            pltpu.VMEM((seq, dh), F32), pltpu.VMEM((seq, dh), F32), pltpu.VMEM((seq, dh), F32),
            pltpu.VMEM((seq, dh), BF16),
            pltpu.VMEM((pad_rows, dh), BF16), pltpu.VMEM((pad_rows, dh), BF16),
            pltpu.VMEM((n_g, seq, dh), F32), pltpu.VMEM((n_g, seq, dh), F32),
        ],
        compiler_params=_params(("parallel", "parallel")),
        name="attn",
    )(*([rest] * (3 * n_g)), cosf, sinf)


def _mix_out_kernel(oa_ref, ob_ref, ga_ref, gb_ref, h_ref, wa_ref, wb_ref, wo_ref, g_ref, b_ref,
                    out_ref, *, alpha):
    ya = _dot(oa_ref[...], wa_ref[...])
    yb = _dot(ob_ref[...], wb_ref[...])
    z = _sigmoid(ga_ref[...].astype(F32)) * ya + _sigmoid(gb_ref[...].astype(F32)) * yb
    mix = _dot(z.astype(BF16), wo_ref[...])
    out_ref[...] = _layer_norm_rows(alpha * h_ref[...] + mix, g_ref[...], b_ref[...])


def mix_out(oa, ob, rest2d, h, wa_bf, wb_bf, wo_bf, g, b, *, alpha, col_gate, tm=256):
    m, d = h.shape
    wa_w = oa.shape[1]
    wb_w = ob.shape[1]
    assert m % tm == 0 and col_gate % d == 0
    const = lambda i: (0, 0)
    single = pl.Buffered(1)
    return pl.pallas_call(
        functools.partial(_mix_out_kernel, alpha=alpha),
        grid=(m // tm,),
        in_specs=[
            pl.BlockSpec((tm, wa_w), lambda i: (i, 0)),
            pl.BlockSpec((tm, wb_w), lambda i: (i, 0)),
            pl.BlockSpec((tm, d), lambda i: (i, col_gate // d)),
            pl.BlockSpec((tm, d), lambda i: (i, col_gate // d + 1)),
            pl.BlockSpec((tm, d), lambda i: (i, 0)),
            pl.BlockSpec((wa_w, d), const, pipeline_mode=single),
            pl.BlockSpec((wb_w, d), const, pipeline_mode=single),
            pl.BlockSpec((d, d), const, pipeline_mode=single),
            pl.BlockSpec((1, d), const),
            pl.BlockSpec((1, d), const),
        ],
        out_specs=pl.BlockSpec((tm, d), lambda i: (i, 0)),
        out_shape=jax.ShapeDtypeStruct((m, d), F32),
        compiler_params=_params(("parallel",)),
        name="mix_out",
    )(oa, ob, rest2d, rest2d, h, wa_bf, wb_bf, wo_bf, g.reshape(1, d), b.reshape(1, d))


def kernel(x, ffn1_w_in, ffn1_w_out, ln1_g, ln1_b, mix_w_in, hgrn_lb_fwd, hgrn_lb_bwd, hgrn_norm_g,
           w_branch_a, w_branch_b, mix_w_out, ln2_g, ln2_b, ffn2_w_in, ffn2_w_out, ln3_g, ln3_b):
    bsz, seq, d = x.shape
    depth = ffn1_w_in.shape[0]
    alpha = (2.0 * depth) ** 0.25
    hw = w_branch_a.shape[1]
    n_heads = hw // HGRN_HEAD_DIM
    qkv_w = len(ATTN_GROUPS) * 3 * ATTN_HEADS_PER_GROUP * HEAD_DIM
    m = bsz * seq

    h = x.reshape(m, d)
    for layer in range(depth):
        h, h_bf = ffn_ln(h, ffn1_w_in[layer].astype(BF16), ffn1_w_out[layer].astype(BF16),
                         ln1_g[layer], ln1_b[layer], alpha=alpha, emit_bf16=True)

        w_in = mix_w_in[layer]
        w_hf = w_in[:, hw:3 * hw].astype(BF16)
        w_rest = jnp.concatenate([w_in[:, :hw], w_in[:, 3 * hw:5 * hw], w_in[:, 5 * hw + qkv_w:],
                                  w_in[:, 5 * hw:5 * hw + qkv_w]], axis=1).astype(BF16)
        hf = matmul(h_bf, w_hf, F32).reshape(bsz, seq, 2 * hw)
        rest2d = matmul(h_bf, w_rest, BF16)
        rest = rest2d.reshape(bsz, seq, -1)
        col_hq, col_hi, col_hog, col_gate = 0, hw, 2 * hw, 3 * hw
        col_qkv = 3 * hw + 2 * d

        o_a = hgrn(rest, hf, hgrn_lb_fwd, hgrn_lb_bwd, hgrn_norm_g[layer], layer=layer,
                   n_heads=n_heads, col_hq=col_hq, col_hi=col_hi, col_hog=col_hog)
        o_b = attn(rest, col_qkv=col_qkv)

        h = mix_out(o_a.reshape(m, hw), o_b.reshape(m, -1), rest2d, h,
                    w_branch_a[layer].astype(BF16), w_branch_b[layer].astype(BF16),
                    mix_w_out[layer].astype(BF16), ln2_g[layer], ln2_b[layer],
                    alpha=alpha, col_gate=col_gate)

        (h,) = ffn_ln(h, ffn2_w_in[layer].astype(BF16), ffn2_w_out[layer].astype(BF16),
                      ln3_g[layer], ln3_b[layer], alpha=alpha, emit_bf16=False)
    return h.reshape(bsz, seq, d)
```

```python
import functools

import jax
import jax.numpy as jnp
from jax import lax
from jax.experimental import pallas as pl
from jax.experimental.pallas import tpu as pltpu

F32 = jnp.float32
BF16 = jnp.bfloat16

HGRN_HEAD_DIM = 128
HGRN_CHUNK = 32
HGRN_SLAB = 4 * HGRN_CHUNK
ATTN_GROUPS = ((128, 1), (512, 4), (2048, 16))
ATTN_HEADS_PER_GROUP = 4
HEAD_DIM = 128
ROPE_THETA = 500000.0
ROPE_DIM = HEAD_DIM // 4
LN_EPS = 1e-5
NEG_INF = -1e30

V7X_VMEM_LIMIT_BYTES = 60000 * 1024


def _params(semantics):
    return pltpu.CompilerParams(dimension_semantics=semantics,
                                vmem_limit_bytes=V7X_VMEM_LIMIT_BYTES)


def _dot(a, b):
    return jnp.dot(a, b, preferred_element_type=F32)


def _dot_nt(a, b):
    return lax.dot_general(a, b, (((1,), (1,)), ((), ())), preferred_element_type=F32)


def _dot_tn(a, b):
    return lax.dot_general(a, b, (((0,), (0,)), ((), ())), preferred_element_type=F32)


def _sigmoid(x):
    return 1.0 / (1.0 + jnp.exp(-x))


def _layer_norm_rows(y, g, b):
    mu = jnp.mean(y, axis=-1, keepdims=True)
    d = y - mu
    var = jnp.mean(d * d, axis=-1, keepdims=True)
    return d * lax.rsqrt(var + LN_EPS) * g + b


def _ffn_ln_kernel(x_ref, wg_ref, wu_ref, wo_ref, g_ref, b_ref, *rest, alpha, n_f):
    (out_ref, *maybe_outbf), (acc_ref, xbf_ref) = rest[:-2], rest[-2:]
    f = pl.program_id(1)

    @pl.when(f == 0)
    def _():
        acc_ref[...] = jnp.zeros_like(acc_ref)
        xbf_ref[...] = x_ref[...].astype(BF16)

    xb = xbf_ref[...]
    gate = _dot(xb, wg_ref[...])
    up = _dot(xb, wu_ref[...])
    hid = (gate * _sigmoid(gate)) * up
    acc_ref[...] += _dot(hid.astype(BF16), wo_ref[...])

    @pl.when(f == n_f - 1)
    def _():
        y = alpha * x_ref[...] + 0.5 * acc_ref[...]
        h = _layer_norm_rows(y, g_ref[...], b_ref[...])
        out_ref[...] = h
        for outbf_ref in maybe_outbf:
            outbf_ref[...] = h.astype(BF16)


def ffn_ln(x, w_in_bf, w_out_bf, g, b, *, alpha, emit_bf16, tm=512, tf=512):
    m, d = x.shape
    ff = w_out_bf.shape[0]
    assert m % tm == 0 and ff % tf == 0
    n_f = ff // tf
    kern = functools.partial(_ffn_ln_kernel, alpha=alpha, n_f=n_f)
    out_dtypes = [F32, BF16] if emit_bf16 else [F32]
    return pl.pallas_call(
        kern,
        grid=(m // tm, n_f),
        in_specs=[
            pl.BlockSpec((tm, d), lambda i, f: (i, 0)),
            pl.BlockSpec((d, tf), lambda i, f: (0, f)),
            pl.BlockSpec((d, tf), lambda i, f: (0, f + n_f)),
            pl.BlockSpec((tf, d), lambda i, f: (f, 0)),
            pl.BlockSpec((1, d), lambda i, f: (0, 0)),
            pl.BlockSpec((1, d), lambda i, f: (0, 0)),
        ],
        out_specs=[pl.BlockSpec((tm, d), lambda i, f: (i, 0)) for _ in out_dtypes],
        out_shape=[jax.ShapeDtypeStruct((m, d), dt) for dt in out_dtypes],
        scratch_shapes=[pltpu.VMEM((tm, d), F32), pltpu.VMEM((tm, d), BF16)],
        compiler_params=_params(("parallel", "arbitrary")),
        name="ffn_ln",
    )(x, w_in_bf, w_in_bf, w_out_bf, g.reshape(1, d), b.reshape(1, d))


def _matmul_kernel(x_ref, w_ref, o_ref):
    o_ref[...] = _dot(x_ref[...], w_ref[...]).astype(o_ref.dtype)


def matmul(x_bf, w_bf, out_dtype, *, tm=2048, tn=512):
    m, k = x_bf.shape
    n = w_bf.shape[1]
    assert m % tm == 0 and n % tn == 0
    return pl.pallas_call(
        _matmul_kernel,
        grid=(m // tm, n // tn),
        in_specs=[pl.BlockSpec((tm, k), lambda i, j: (i, 0)),
                  pl.BlockSpec((k, tn), lambda i, j: (0, j))],
        out_specs=pl.BlockSpec((tm, tn), lambda i, j: (i, j)),
        out_shape=jax.ShapeDtypeStruct((m, n), out_dtype),
        compiler_params=_params(("parallel", "arbitrary")),
        name="in_proj",
    )(x_bf, w_bf)


def _split2(x):
    hi = x.astype(BF16)
    lo = (x - hi.astype(F32)).astype(BF16)
    return hi, lo


def _hgrn_kernel(hq_ref, hff_ref, hfb_ref, hi_ref, hog_ref, lbf_ref, lbb_ref, ng_ref, out_ref,
                 qd_ref, qmix_ref, q64_ref, q128_ref, kdt_ref, kmixt_ref, k64t_ref,
                 dec_ref, kv_ref, snap_ref, *, seq, layer):
    c = HGRN_CHUNK
    slab = HGRN_SLAB
    n_slab = seq // slab
    dh = HGRN_HEAD_DIM
    assert slab == 4 * c

    def lower_bound(lb_ref):
        t = lb_ref[...].astype(F32)
        e = jnp.exp(t - jnp.max(t, axis=0, keepdims=True))
        sm = e / jnp.sum(e, axis=0, keepdims=True)
        return jnp.sum(sm[: layer + 1], axis=0, keepdims=True)

    lbs = (lower_bound(lbf_ref), lower_bound(lbb_ref))
    hf_refs = (hff_ref, hfb_ref)

    ri = lax.broadcasted_iota(jnp.int32, (slab, slab), 0)
    ci = lax.broadcasted_iota(jnp.int32, (slab, slab), 1)
    same_chunk = (ri // c) == (ci // c)
    same_pair = (ri // (2 * c)) == (ci // (2 * c))
    lower = ci <= ri
    upper = ci >= ri
    tmats = tuple(jnp.where(same_chunk & m, 1.0, 0.0).astype(BF16) for m in (lower, upper))

    def prep(s, carry):
        r0 = pl.multiple_of(s * slab, slab)
        rows = pl.ds(r0, slab)
        q = hq_ref[rows, :].astype(F32)
        q = q * _sigmoid(q)
        qd, ke, e = [], [], []
        for d in range(2):
            f = lbs[d] + (1.0 - lbs[d]) * _sigmoid(hf_refs[d][rows, :].astype(F32))
            hi, lo = _split2(jnp.log(f))
            cum = _dot(tmats[d], hi) + _dot(tmats[d], lo)
            tot_row = c - 1 if d == 0 else 0
            e.append([jnp.exp(cum[j * c + tot_row: j * c + tot_row + 1, :]) for j in range(4)])
            ecum = jnp.exp(cum)
            qd_d = q * ecum
            kd_d = (1.0 - f) / ecum
            qd_ref[d, rows, :] = qd_d.astype(BF16)
            kdt_ref[d, s] = kd_d.T.astype(BF16)
            qd.append([qd_d[j * c:(j + 1) * c] for j in range(4)])
            ke.append([kd_d[j * c:(j + 1) * c] * e[d][j] for j in range(4)])

        cat = lambda pieces: jnp.concatenate(pieces, axis=0)
        qmix_ref[rows, :] = cat([qd[1][0], qd[0][1], qd[1][2], qd[0][3]]).astype(BF16)
        kmixt_ref[s] = cat([ke[0][0], ke[1][1], ke[0][2], ke[1][3]]).T.astype(BF16)
        q64_ref[rows, :] = cat([qd[1][0] * e[1][1], qd[1][1], qd[0][2], qd[0][3] * e[0][2]]).astype(BF16)
        k64t_ref[s] = cat([ke[0][0] * e[0][1], ke[0][1], ke[1][2], ke[1][3] * e[1][2]]).T.astype(BF16)
        k128 = []
        for d in range(2):
            e0, e1, e2, e3 = e[d]
            from_start = (None, e0, e0 * e1, e0 * e1 * e2)
            to_end = (e1 * e2 * e3, e2 * e3, e3, None)
            qmul, kmul = (from_start, to_end) if d == 0 else (to_end, from_start)
            mul = lambda x, m: x if m is None else x * m
            q128_ref[rows, d * dh:(d + 1) * dh] = cat([mul(qd[d][j], qmul[j]) for j in range(4)]).astype(BF16)
            k128.append(cat([mul(ke[d][j], kmul[j]) for j in range(4)]).astype(BF16))
            dec_ref[d, pl.ds(pl.multiple_of(s * 8, 8), 8), :] = jnp.broadcast_to(e0 * e1 * e2 * e3, (8, dh))
        kv_ref[s] = _dot_tn(hi_ref[rows, :], jnp.concatenate(k128, axis=1))
        return carry

    lax.fori_loop(0, n_slab, prep, 0, unroll=16)

    def state_step(i, st, d):
        s = i if d == 0 else n_slab - 1 - i
        snap_ref[s, :, d * dh:(d + 1) * dh] = st
        return st * dec_ref[d, pl.ds(s * 8, 1), :] + kv_ref[s, :, d * dh:(d + 1) * dh]

    zero_state = jnp.zeros((dh, dh), F32)
    for d in range(2):
        lax.fori_loop(0, n_slab, functools.partial(state_step, d=d), zero_state)

    def out_slab(s, carry):
        rows = pl.ds(pl.multiple_of(s * slab, slab), slab)
        m_df = _dot(qd_ref[0, rows, :], kdt_ref[0, s])
        m_db = _dot(qd_ref[1, rows, :], kdt_ref[1, s])
        m_32 = _dot(qmix_ref[rows, :], kmixt_ref[s])
        m_64 = _dot(q64_ref[rows, :], k64t_ref[s])
        a = jnp.where(same_chunk,
                      jnp.where(lower, m_df, 0.0) + jnp.where(upper, m_db, 0.0),
                      jnp.where(same_pair, m_32, m_64))
        states = jnp.concatenate([snap_ref[s, :, :dh].T, snap_ref[s, :, dh:].T], axis=0)
        o = _dot(a.astype(BF16), hi_ref[rows, :]) + _dot(q128_ref[rows, :], states.astype(BF16))
        o = o * lax.rsqrt(jnp.mean(o * o, axis=-1, keepdims=True) + LN_EPS)
        og = hog_ref[rows, :].astype(F32)
        out_ref[rows, :] = (o * ng_ref[...] * (og * _sigmoid(og))).astype(out_ref.dtype)
        return carry

    lax.fori_loop(0, n_slab, out_slab, 0, unroll=16)


def hgrn(rest, hf, lb_fwd, lb_bwd, norm_g, *, layer, n_heads, col_hq, col_hi, col_hog):
    bsz, seq, _ = rest.shape
    dh = HGRN_HEAD_DIM
    width = n_heads * dh
    nl = lb_fwd.shape[0]
    slab = HGRN_SLAB
    assert seq % slab == 0
    n_slab = seq // slab
    kern = functools.partial(_hgrn_kernel, seq=seq, layer=layer)

    def col(base):
        return lambda b, h: (b, 0, base // dh + h)

    return pl.pallas_call(
        kern,
        grid=(bsz, n_heads),
        in_specs=[
            pl.BlockSpec((None, seq, dh), col(col_hq)),
            pl.BlockSpec((None, seq, dh), col(0)),
            pl.BlockSpec((None, seq, dh), col(width)),
            pl.BlockSpec((None, seq, dh), col(col_hi)),
            pl.BlockSpec((None, seq, dh), col(col_hog)),
            pl.BlockSpec((nl, dh), lambda b, h: (0, h)),
            pl.BlockSpec((nl, dh), lambda b, h: (0, h)),
            pl.BlockSpec((1, dh), lambda b, h: (0, h)),
        ],
        out_specs=pl.BlockSpec((None, seq, dh), lambda b, h: (b, 0, h)),
        out_shape=jax.ShapeDtypeStruct((bsz, seq, width), BF16),
        scratch_shapes=[
            pltpu.VMEM((2, seq, dh), BF16),
            pltpu.VMEM((seq, dh), BF16),
            pltpu.VMEM((seq, dh), BF16),
            pltpu.VMEM((seq, 2 * dh), BF16),
            pltpu.VMEM((2, n_slab, dh, slab), BF16),
            pltpu.VMEM((n_slab, dh, slab), BF16),
            pltpu.VMEM((n_slab, dh, slab), BF16),
            pltpu.VMEM((2, n_slab * 8, dh), F32),
            pltpu.VMEM((n_slab, dh, 2 * dh), F32),
            pltpu.VMEM((n_slab, dh, 2 * dh), F32),
        ],
        compiler_params=_params(("parallel", "parallel")),
        name="hgrn",
    )(rest, hf, hf, rest, rest, lb_fwd, lb_bwd, norm_g.reshape(1, width))


ATT_TQ = 128
ATT_HALF = 64
ATT_TK = ATT_TQ + 2 * ATT_HALF


def _attn_kernel(*refs, seq):
    n_g = len(ATTN_GROUPS)
    qkv_refs = refs[: 3 * n_g]
    cos_ref, sin_ref, out_ref = refs[3 * n_g: 3 * n_g + 3]
    qr_ref, kr_ref, vf_ref, qcm_ref, kcm_ref, vcm_ref, og_ref, lse_ref = refs[3 * n_g + 3:]
    dh = HEAD_DIM
    scale = HEAD_DIM ** -0.5

    lane = lax.broadcasted_iota(jnp.int32, (ATT_TK, dh), 1)

    def rope(x_ref, rows):
        t = x_ref[rows, :].astype(F32)
        swapped = jnp.where(lane < ROPE_DIM // 2,
                            pltpu.roll(t, dh - ROPE_DIM // 2, 1),
                            pltpu.roll(t, ROPE_DIM // 2, 1))
        return t * cos_ref[rows, :] + swapped * sin_ref[rows, :]

    qi_rel = lax.broadcasted_iota(jnp.int32, (ATT_TQ, ATT_TK), 0)
    kj_rel = lax.broadcasted_iota(jnp.int32, (ATT_TQ, ATT_TK), 1) - ATT_HALF
    band = jnp.abs(qi_rel - kj_rel) <= ATT_HALF

    for g, (window, dil) in enumerate(ATTN_GROUPS):
        assert window // (2 * dil) == ATT_HALF
        seg = seq // dil
        n_blk = seg // ATT_TQ
        assert n_blk & (n_blk - 1) == 0
        pitch = seg + ATT_TK
        q_ref, k_ref, v_ref = qkv_refs[3 * g: 3 * g + 3]

        def stage(s, carry, dil=dil, q_ref=q_ref, k_ref=k_ref, v_ref=v_ref):
            rows = pl.ds(pl.multiple_of(s * ATT_TK, ATT_TK), ATT_TK)
            q = rope(q_ref, rows)
            k = rope(k_ref, rows)
            if dil == 1:
                pad_rows = pl.ds(pl.multiple_of(s * ATT_TK + ATT_HALF, ATT_HALF), ATT_TK)
                qcm_ref[rows, :] = q.astype(BF16)
                kcm_ref[pad_rows, :] = k.astype(BF16)
                vcm_ref[pad_rows, :] = v_ref[rows, :]
            else:
                qr_ref[rows, :] = q
                kr_ref[rows, :] = k
                vf_ref[rows, :] = v_ref[rows, :].astype(F32)
            return carry

        lax.fori_loop(0, seq // ATT_TK, stage, 0, unroll=4)
        for cls in range(dil):
            base = cls * pitch
            for pad_ref, src_ref in ((kcm_ref, kr_ref), (vcm_ref, vf_ref)):
                pad_ref[pl.ds(base, ATT_HALF), :] = jnp.zeros((ATT_HALF, dh), BF16)
                pad_ref[pl.ds(base + ATT_HALF + seg, ATT_TK - ATT_HALF), :] = (
                    jnp.zeros((ATT_TK - ATT_HALF, dh), BF16))
                if dil > 1:
                    pad_ref[pl.ds(base + ATT_HALF, seg), :] = (
                        src_ref[pl.ds(cls, seg, stride=dil), :].astype(BF16))
            if dil > 1:
                qcm_ref[pl.ds(cls * seg, seg), :] = qr_ref[pl.ds(cls, seg, stride=dil), :].astype(BF16)

        def block(u, carry, g=g, dil=dil, seg=seg, n_blk=n_blk, pitch=pitch):
            cls = lax.shift_right_logical(u, n_blk.bit_length() - 1)
            q0 = (u & (n_blk - 1)) * ATT_TQ
            if dil == 1:
                tok_rows = pl.ds(pl.multiple_of(q0, ATT_TQ), ATT_TQ)
            else:
                tok_rows = pl.ds(cls + dil * q0, ATT_TQ, stride=dil)
            qb = qcm_ref[pl.ds(pl.multiple_of(u * ATT_TQ, ATT_TQ), ATT_TQ), :]
            win = pl.ds(pl.multiple_of(cls * pitch + q0, ATT_TQ), ATT_TK)
            s = _dot_nt(qb, kcm_ref[win, :]) * scale
            kj = kj_rel + q0
            valid = band & (kj >= 0) & (kj < seg)
            s = jnp.where(valid, s, NEG_INF)
            m = jnp.max(s, axis=-1, keepdims=True)
            p = jnp.exp(s - m)
            denom = jnp.sum(p, axis=-1, keepdims=True)
            o = _dot(p.astype(BF16), vcm_ref[win, :]) / denom
            lse = m + jnp.log(denom)
            og_ref[g, tok_rows, :] = o
            lse_ref[g, tok_rows, :] = jnp.broadcast_to(lse, (ATT_TQ, dh))
            return carry

        lax.fori_loop(0, dil * n_blk, block, 0, unroll=16)

    def combine(s, carry):
        rows = pl.ds(pl.multiple_of(s * ATT_TK, ATT_TK), ATT_TK)
        lses = [lse_ref[g, rows, :] for g in range(n_g)]
        mx = functools.reduce(jnp.maximum, lses)
        ws = [jnp.exp(l - mx) for l in lses]
        wsum = functools.reduce(lambda a, b: a + b, ws)
        acc = ws[0] * og_ref[0, rows, :]
        for g in range(1, n_g):
            acc = acc + ws[g] * og_ref[g, rows, :]
        out_ref[rows, :] = (acc / wsum).astype(out_ref.dtype)
        return carry

    lax.fori_loop(0, seq // ATT_TK, combine, 0, unroll=2)


def _rope_tables(seq):
    inv_freq = ROPE_THETA ** (-jnp.arange(0, ROPE_DIM, 2, dtype=F32) / ROPE_DIM)
    ang = jnp.arange(seq).astype(F32)[:, None] * inv_freq
    cos, sin = jnp.cos(ang), jnp.sin(ang)
    pad1 = jnp.ones((seq, HEAD_DIM - ROPE_DIM), F32)
    pad0 = jnp.zeros((seq, HEAD_DIM - ROPE_DIM), F32)
    return (jnp.concatenate([cos, cos, pad1], axis=1),
            jnp.concatenate([-sin, sin, pad0], axis=1))


def attn(rest, *, col_qkv):
    bsz, seq, _ = rest.shape
    dh = HEAD_DIM
    hpg = ATTN_HEADS_PER_GROUP
    n_g = len(ATTN_GROUPS)
    assert seq % (max(d for _, d in ATTN_GROUPS) * ATT_TQ) == 0
    cosf, sinf = _rope_tables(seq)

    def col(g, t):
        base = col_qkv // dh + (g * 3 + t) * hpg
        return lambda b, j: (b, 0, base + j)

    in_specs = [pl.BlockSpec((None, seq, dh), col(g, t)) for g in range(n_g) for t in range(3)]
    in_specs += [pl.BlockSpec((seq, dh), lambda b, j: (0, 0)) for _ in range(2)]
    pad_rows = max(dil * (seq // dil + ATT_TK) for _, dil in ATTN_GROUPS)
    return pl.pallas_call(
        functools.partial(_attn_kernel, seq=seq),
        grid=(bsz, hpg),
        in_specs=in_specs,
        out_specs=pl.BlockSpec((None, seq, dh), lambda b, j: (b, 0, j)),
        out_shape=jax.ShapeDtypeStruct((bsz, seq, hpg * dh), BF16),
        scratch_shapes=[
            pltpu.VMEM((seq, dh), F32), pltpu.VMEM((seq, dh), F32), pltpu.VMEM((seq, dh), F32),
            pltpu.VMEM((seq, dh), BF16),
            pltpu.VMEM((pad_rows, dh), BF16), pltpu.VMEM((pad_rows, dh), BF16),
            pltpu.VMEM((n_g, seq, dh), F32), pltpu.VMEM((n_g, seq, dh), F32),
        ],
        compiler_params=_params(("parallel", "parallel")),
        name="attn",
    )(*([rest] * (3 * n_g)), cosf, sinf)


def _mix_out_kernel(oa_ref, ob_ref, ga_ref, gb_ref, h_ref, wa_ref, wb_ref, wo_ref, g_ref, b_ref,
                    out_ref, *, alpha):
    ya = _dot(oa_ref[...], wa_ref[...])
    yb = _dot(ob_ref[...], wb_ref[...])
    z = _sigmoid(ga_ref[...].astype(F32)) * ya + _sigmoid(gb_ref[...].astype(F32)) * yb
    mix = _dot(z.astype(BF16), wo_ref[...])
    out_ref[...] = _layer_norm_rows(alpha * h_ref[...] + mix, g_ref[...], b_ref[...])


def mix_out(oa, ob, rest2d, h, wa_bf, wb_bf, wo_bf, g, b, *, alpha, col_gate, tm=256):
    m, d = h.shape
    wa_w = oa.shape[1]
    wb_w = ob.shape[1]
    assert m % tm == 0 and col_gate % d == 0
    const = lambda i: (0, 0)
    single = pl.Buffered(1)
    return pl.pallas_call(
        functools.partial(_mix_out_kernel, alpha=alpha),
        grid=(m // tm,),
        in_specs=[
            pl.BlockSpec((tm, wa_w), lambda i: (i, 0)),
            pl.BlockSpec((tm, wb_w), lambda i: (i, 0)),
            pl.BlockSpec((tm, d), lambda i: (i, col_gate // d)),
            pl.BlockSpec((tm, d), lambda i: (i, col_gate // d + 1)),
            pl.BlockSpec((tm, d), lambda i: (i, 0)),
            pl.BlockSpec((wa_w, d), const, pipeline_mode=single),
            pl.BlockSpec((wb_w, d), const, pipeline_mode=single),
            pl.BlockSpec((d, d), const, pipeline_mode=single),
            pl.BlockSpec((1, d), const),
            pl.BlockSpec((1, d), const),
        ],
        out_specs=pl.BlockSpec((tm, d), lambda i: (i, 0)),
        out_shape=jax.ShapeDtypeStruct((m, d), F32),
        compiler_params=_params(("parallel",)),
        name="mix_out",
    )(oa, ob, rest2d, rest2d, h, wa_bf, wb_bf, wo_bf, g.reshape(1, d), b.reshape(1, d))


def kernel(x, ffn1_w_in, ffn1_w_out, ln1_g, ln1_b, mix_w_in, hgrn_lb_fwd, hgrn_lb_bwd, hgrn_norm_g,
           w_branch_a, w_branch_b, mix_w_out, ln2_g, ln2_b, ffn2_w_in, ffn2_w_out, ln3_g, ln3_b):
    bsz, seq, d = x.shape
    depth = ffn1_w_in.shape[0]
    alpha = (2.0 * depth) ** 0.25
    hw = w_branch_a.shape[1]
    n_heads = hw // HGRN_HEAD_DIM
    qkv_w = len(ATTN_GROUPS) * 3 * ATTN_HEADS_PER_GROUP * HEAD_DIM
    m = bsz * seq

    h = x.reshape(m, d)
    for layer in range(depth):
        h, h_bf = ffn_ln(h, ffn1_w_in[layer].astype(BF16), ffn1_w_out[layer].astype(BF16),
                         ln1_g[layer], ln1_b[layer], alpha=alpha, emit_bf16=True)

        w_in = mix_w_in[layer]
        w_hf = w_in[:, hw:3 * hw].astype(BF16)
        w_rest = jnp.concatenate([w_in[:, :hw], w_in[:, 3 * hw:5 * hw], w_in[:, 5 * hw + qkv_w:],
                                  w_in[:, 5 * hw:5 * hw + qkv_w]], axis=1).astype(BF16)
        hf = matmul(h_bf, w_hf, F32).reshape(bsz, seq, 2 * hw)
        rest2d = matmul(h_bf, w_rest, BF16, tm=1024, tn=w_rest.shape[1] // 4)
        rest = rest2d.reshape(bsz, seq, -1)
        col_hq, col_hi, col_hog, col_gate = 0, hw, 2 * hw, 3 * hw
        col_qkv = 3 * hw + 2 * d

        o_a = hgrn(rest, hf, hgrn_lb_fwd, hgrn_lb_bwd, hgrn_norm_g[layer], layer=layer,
                   n_heads=n_heads, col_hq=col_hq, col_hi=col_hi, col_hog=col_hog)
        o_b = attn(rest, col_qkv=col_qkv)

        h = mix_out(o_a.reshape(m, hw), o_b.reshape(m, -1), rest2d, h,
                    w_branch_a[layer].astype(BF16), w_branch_b[layer].astype(BF16),
                    mix_w_out[layer].astype(BF16), ln2_g[layer], ln2_b[layer],
                    alpha=alpha, col_gate=col_gate)

        (h,) = ffn_ln(h, ffn2_w_in[layer].astype(BF16), ffn2_w_out[layer].astype(BF16),
                      ln3_g[layer], ln3_b[layer], alpha=alpha, emit_bf16=False)
    return h.reshape(bsz, seq, d)
```

```python
import functools

import jax
import jax.numpy as jnp
import numpy as np
from jax import lax
from jax.experimental import pallas as pl
from jax.experimental.pallas import tpu as pltpu

F32 = jnp.float32
BF16 = jnp.bfloat16

HGRN_HEAD_DIM = 128
HGRN_CHUNK = 32
HGRN_SLAB = 4 * HGRN_CHUNK
ATTN_GROUPS = ((128, 1), (512, 4), (2048, 16))
ATTN_HEADS_PER_GROUP = 4
HEAD_DIM = 128
ROPE_THETA = 500000.0
ROPE_DIM = HEAD_DIM // 4
LN_EPS = 1e-5
NEG_INF = -1e30

V7X_VMEM_LIMIT_BYTES = 60000 * 1024


def _params(semantics):
    return pltpu.CompilerParams(dimension_semantics=semantics,
                                vmem_limit_bytes=V7X_VMEM_LIMIT_BYTES)


def _dot(a, b):
    return jnp.dot(a, b, preferred_element_type=F32)


def _dot_nt(a, b):
    return lax.dot_general(a, b, (((1,), (1,)), ((), ())), preferred_element_type=F32)


def _dot_tn(a, b):
    return lax.dot_general(a, b, (((0,), (0,)), ((), ())), preferred_element_type=F32)


def _sigmoid(x):
    return 1.0 / (1.0 + jnp.exp(-x))


def _layer_norm_rows(y, g, b):
    mu = jnp.mean(y, axis=-1, keepdims=True)
    d = y - mu
    var = jnp.mean(d * d, axis=-1, keepdims=True)
    return d * lax.rsqrt(var + LN_EPS) * g + b


def _ffn_ln_kernel(x_ref, wg_ref, wu_ref, wo_ref, g_ref, b_ref, *rest, alpha, n_f):
    (out_ref, *maybe_outbf), (acc_ref, xbf_ref) = rest[:-2], rest[-2:]
    f = pl.program_id(1)

    @pl.when(f == 0)
    def _():
        acc_ref[...] = jnp.zeros_like(acc_ref)
        xbf_ref[...] = x_ref[...].astype(BF16)

    xb = xbf_ref[...]
    gate = _dot(xb, wg_ref[...])
    up = _dot(xb, wu_ref[...])
    hid = (gate * _sigmoid(gate)) * up
    acc_ref[...] += _dot(hid.astype(BF16), wo_ref[...])

    @pl.when(f == n_f - 1)
    def _():
        y = alpha * x_ref[...] + 0.5 * acc_ref[...]
        h = _layer_norm_rows(y, g_ref[...], b_ref[...])
        out_ref[...] = h
        for outbf_ref in maybe_outbf:
            outbf_ref[...] = h.astype(BF16)


def ffn_ln(x, w_in_bf, w_out_bf, g, b, *, alpha, emit_bf16, tm=512, tf=512):
    m, d = x.shape
    ff = w_out_bf.shape[0]
    assert m % tm == 0 and ff % tf == 0
    n_f = ff // tf
    kern = functools.partial(_ffn_ln_kernel, alpha=alpha, n_f=n_f)
    out_dtypes = [F32, BF16] if emit_bf16 else [F32]
    return pl.pallas_call(
        kern,
        grid=(m // tm, n_f),
        in_specs=[
            pl.BlockSpec((tm, d), lambda i, f: (i, 0)),
            pl.BlockSpec((d, tf), lambda i, f: (0, f)),
            pl.BlockSpec((d, tf), lambda i, f: (0, f + n_f)),
            pl.BlockSpec((tf, d), lambda i, f: (f, 0)),
            pl.BlockSpec((1, d), lambda i, f: (0, 0)),
            pl.BlockSpec((1, d), lambda i, f: (0, 0)),
        ],
        out_specs=[pl.BlockSpec((tm, d), lambda i, f: (i, 0)) for _ in out_dtypes],
        out_shape=[jax.ShapeDtypeStruct((m, d), dt) for dt in out_dtypes],
        scratch_shapes=[pltpu.VMEM((tm, d), F32), pltpu.VMEM((tm, d), BF16)],
        compiler_params=_params(("parallel", "arbitrary")),
        name="ffn_ln",
    )(x, w_in_bf, w_in_bf, w_out_bf, g.reshape(1, d), b.reshape(1, d))


def _matmul_kernel(x_ref, w_ref, o_ref):
    o_ref[...] = _dot(x_ref[...], w_ref[...]).astype(o_ref.dtype)


def matmul(x_bf, w_bf, out_dtype, *, tm=2048, tn=512):
    m, k = x_bf.shape
    n = w_bf.shape[1]
    assert m % tm == 0 and n % tn == 0
    return pl.pallas_call(
        _matmul_kernel,
        grid=(m // tm, n // tn),
        in_specs=[pl.BlockSpec((tm, k), lambda i, j: (i, 0)),
                  pl.BlockSpec((k, tn), lambda i, j: (0, j))],
        out_specs=pl.BlockSpec((tm, tn), lambda i, j: (i, j)),
        out_shape=jax.ShapeDtypeStruct((m, n), out_dtype),
        compiler_params=_params(("parallel", "arbitrary")),
        name="in_proj",
    )(x_bf, w_bf)


def _split2(x):
    hi = x.astype(BF16)
    lo = (x - hi.astype(F32)).astype(BF16)
    return hi, lo


def _hgrn_kernel(hq_ref, hff_ref, hfb_ref, hi_ref, hog_ref, lbf_ref, lbb_ref, ng_ref, out_ref,
                 qd_ref, qmix_ref, q64_ref, q128_ref, kdt_ref, kmixt_ref, k64t_ref,
                 dec_ref, kv_ref, snap_ref, *, seq, layer):
    c = HGRN_CHUNK
    slab = HGRN_SLAB
    n_slab = seq // slab
    dh = HGRN_HEAD_DIM
    assert slab == 4 * c

    def lower_bound(lb_ref):
        t = lb_ref[...].astype(F32)
        e = jnp.exp(t - jnp.max(t, axis=0, keepdims=True))
        sm = e / jnp.sum(e, axis=0, keepdims=True)
        return jnp.sum(sm[: layer + 1], axis=0, keepdims=True)

    lbs = (lower_bound(lbf_ref), lower_bound(lbb_ref))
    hf_refs = (hff_ref, hfb_ref)

    ri = lax.broadcasted_iota(jnp.int32, (slab, slab), 0)
    ci = lax.broadcasted_iota(jnp.int32, (slab, slab), 1)
    same_chunk = (ri // c) == (ci // c)
    same_pair = (ri // (2 * c)) == (ci // (2 * c))
    lower = ci <= ri
    upper = ci >= ri
    tmats = tuple(jnp.where(same_chunk & m, 1.0, 0.0).astype(BF16) for m in (lower, upper))

    def prep(s, carry):
        r0 = pl.multiple_of(s * slab, slab)
        rows = pl.ds(r0, slab)
        q = hq_ref[rows, :].astype(F32)
        q = q * _sigmoid(q)
        qd, ke, e = [], [], []
        for d in range(2):
            f = lbs[d] + (1.0 - lbs[d]) * _sigmoid(hf_refs[d][rows, :].astype(F32))
            hi, lo = _split2(jnp.log(f))
            cum = _dot(tmats[d], hi) + _dot(tmats[d], lo)
            tot_row = c - 1 if d == 0 else 0
            e.append([jnp.exp(cum[j * c + tot_row: j * c + tot_row + 1, :]) for j in range(4)])
            ecum = jnp.exp(cum)
            qd_d = q * ecum
            kd_d = (1.0 - f) / ecum
            qd_ref[d, rows, :] = qd_d.astype(BF16)
            kdt_ref[d, s] = kd_d.T.astype(BF16)
            qd.append([qd_d[j * c:(j + 1) * c] for j in range(4)])
            ke.append([kd_d[j * c:(j + 1) * c] * e[d][j] for j in range(4)])

        cat = lambda pieces: jnp.concatenate(pieces, axis=0)
        qmix_ref[rows, :] = cat([qd[1][0], qd[0][1], qd[1][2], qd[0][3]]).astype(BF16)
        kmixt_ref[s] = cat([ke[0][0], ke[1][1], ke[0][2], ke[1][3]]).T.astype(BF16)
        q64_ref[rows, :] = cat([qd[1][0] * e[1][1], qd[1][1], qd[0][2], qd[0][3] * e[0][2]]).astype(BF16)
        k64t_ref[s] = cat([ke[0][0] * e[0][1], ke[0][1], ke[1][2], ke[1][3] * e[1][2]]).T.astype(BF16)
        k128 = []
        for d in range(2):
            e0, e1, e2, e3 = e[d]
            from_start = (None, e0, e0 * e1, e0 * e1 * e2)
            to_end = (e1 * e2 * e3, e2 * e3, e3, None)
            qmul, kmul = (from_start, to_end) if d == 0 else (to_end, from_start)
            mul = lambda x, m: x if m is None else x * m
            q128_ref[rows, d * dh:(d + 1) * dh] = cat([mul(qd[d][j], qmul[j]) for j in range(4)]).astype(BF16)
            k128.append(cat([mul(ke[d][j], kmul[j]) for j in range(4)]).astype(BF16))
            dec_ref[d, pl.ds(pl.multiple_of(s * 8, 8), 8), :] = jnp.broadcast_to(e0 * e1 * e2 * e3, (8, dh))
        kv_ref[s] = _dot_tn(hi_ref[rows, :], jnp.concatenate(k128, axis=1))
        return carry

    lax.fori_loop(0, n_slab, prep, 0, unroll=16)

    def state_step(i, st, d):
        s = i if d == 0 else n_slab - 1 - i
        snap_ref[s, :, d * dh:(d + 1) * dh] = st
        return st * dec_ref[d, pl.ds(s * 8, 1), :] + kv_ref[s, :, d * dh:(d + 1) * dh]

    zero_state = jnp.zeros((dh, dh), F32)
    for d in range(2):
        lax.fori_loop(0, n_slab, functools.partial(state_step, d=d), zero_state)

    def out_slab(s, carry):
        rows = pl.ds(pl.multiple_of(s * slab, slab), slab)
        m_df = _dot(qd_ref[0, rows, :], kdt_ref[0, s])
        m_db = _dot(qd_ref[1, rows, :], kdt_ref[1, s])
        m_32 = _dot(qmix_ref[rows, :], kmixt_ref[s])
        m_64 = _dot(q64_ref[rows, :], k64t_ref[s])
        a = jnp.where(same_chunk,
                      jnp.where(lower, m_df, 0.0) + jnp.where(upper, m_db, 0.0),
                      jnp.where(same_pair, m_32, m_64))
        states = jnp.concatenate([snap_ref[s, :, :dh].T, snap_ref[s, :, dh:].T], axis=0)
        o = _dot(a.astype(BF16), hi_ref[rows, :]) + _dot(q128_ref[rows, :], states.astype(BF16))
        o = o * lax.rsqrt(jnp.mean(o * o, axis=-1, keepdims=True) + LN_EPS)
        og = hog_ref[rows, :].astype(F32)
        out_ref[rows, :] = (o * ng_ref[...] * (og * _sigmoid(og))).astype(out_ref.dtype)
        return carry

    lax.fori_loop(0, n_slab, out_slab, 0, unroll=16)


def hgrn(rest, hf, lb_fwd, lb_bwd, norm_g, *, layer, n_heads, col_hq, col_hi, col_hog):
    bsz, seq, _ = rest.shape
    dh = HGRN_HEAD_DIM
    width = n_heads * dh
    nl = lb_fwd.shape[0]
    slab = HGRN_SLAB
    assert seq % slab == 0
    n_slab = seq // slab
    kern = functools.partial(_hgrn_kernel, seq=seq, layer=layer)

    def col(base):
        return lambda b, h: (b, 0, base // dh + h)

    return pl.pallas_call(
        kern,
        grid=(bsz, n_heads),
        in_specs=[
            pl.BlockSpec((None, seq, dh), col(col_hq)),
            pl.BlockSpec((None, seq, dh), col(0)),
            pl.BlockSpec((None, seq, dh), col(width)),
            pl.BlockSpec((None, seq, dh), col(col_hi)),
            pl.BlockSpec((None, seq, dh), col(col_hog)),
            pl.BlockSpec((nl, dh), lambda b, h: (0, h)),
            pl.BlockSpec((nl, dh), lambda b, h: (0, h)),
            pl.BlockSpec((1, dh), lambda b, h: (0, h)),
        ],
        out_specs=pl.BlockSpec((None, seq, dh), lambda b, h: (b, 0, h)),
        out_shape=jax.ShapeDtypeStruct((bsz, seq, width), BF16),
        scratch_shapes=[
            pltpu.VMEM((2, seq, dh), BF16),
            pltpu.VMEM((seq, dh), BF16),
            pltpu.VMEM((seq, dh), BF16),
            pltpu.VMEM((seq, 2 * dh), BF16),
            pltpu.VMEM((2, n_slab, dh, slab), BF16),
            pltpu.VMEM((n_slab, dh, slab), BF16),
            pltpu.VMEM((n_slab, dh, slab), BF16),
            pltpu.VMEM((2, n_slab * 8, dh), F32),
            pltpu.VMEM((n_slab, dh, 2 * dh), F32),
            pltpu.VMEM((n_slab, dh, 2 * dh), F32),
        ],
        compiler_params=_params(("parallel", "parallel")),
        name="hgrn",
    )(rest, hf, hf, rest, rest, lb_fwd, lb_bwd, norm_g.reshape(1, width))


ATT_TQ = 128
ATT_HALF = 64
ATT_TK = ATT_TQ + 2 * ATT_HALF


def _attn_kernel(*refs, seq):
    n_g = len(ATTN_GROUPS)
    qkv_refs = refs[: 3 * n_g]
    cos_ref, sin_ref, out_ref = refs[3 * n_g: 3 * n_g + 3]
    qr_ref, kr_ref, vf_ref, qcm_ref, kcm_ref, vcm_ref, og_ref, lse_ref = refs[3 * n_g + 3:]
    dh = HEAD_DIM
    scale = HEAD_DIM ** -0.5

    lane = lax.broadcasted_iota(jnp.int32, (ATT_TK, dh), 1)

    def rope(x_ref, rows):
        t = x_ref[rows, :].astype(F32)
        swapped = jnp.where(lane < ROPE_DIM // 2,
                            pltpu.roll(t, dh - ROPE_DIM // 2, 1),
                            pltpu.roll(t, ROPE_DIM // 2, 1))
        return t * cos_ref[rows, :] + swapped * sin_ref[rows, :]

    qi_rel = lax.broadcasted_iota(jnp.int32, (ATT_TQ, ATT_TK), 0)
    kj_rel = lax.broadcasted_iota(jnp.int32, (ATT_TQ, ATT_TK), 1) - ATT_HALF
    band = jnp.abs(qi_rel - kj_rel) <= ATT_HALF

    for g, (window, dil) in enumerate(ATTN_GROUPS):
        assert window // (2 * dil) == ATT_HALF
        seg = seq // dil
        n_blk = seg // ATT_TQ
        assert n_blk & (n_blk - 1) == 0
        pitch = seg + ATT_TK
        q_ref, k_ref, v_ref = qkv_refs[3 * g: 3 * g + 3]

        def stage(s, carry, dil=dil, q_ref=q_ref, k_ref=k_ref, v_ref=v_ref):
            rows = pl.ds(pl.multiple_of(s * ATT_TK, ATT_TK), ATT_TK)
            q = rope(q_ref, rows)
            k = rope(k_ref, rows)
            if dil == 1:
                pad_rows = pl.ds(pl.multiple_of(s * ATT_TK + ATT_HALF, ATT_HALF), ATT_TK)
                qcm_ref[rows, :] = q.astype(BF16)
                kcm_ref[pad_rows, :] = k.astype(BF16)
                vcm_ref[pad_rows, :] = v_ref[rows, :]
            else:
                qr_ref[rows, :] = q
                kr_ref[rows, :] = k
                vf_ref[rows, :] = v_ref[rows, :].astype(F32)
            return carry

        lax.fori_loop(0, seq // ATT_TK, stage, 0, unroll=4)
        for cls in range(dil):
            base = cls * pitch
            for pad_ref, src_ref in ((kcm_ref, kr_ref), (vcm_ref, vf_ref)):
                pad_ref[pl.ds(base, ATT_HALF), :] = jnp.zeros((ATT_HALF, dh), BF16)
                pad_ref[pl.ds(base + ATT_HALF + seg, ATT_TK - ATT_HALF), :] = (
                    jnp.zeros((ATT_TK - ATT_HALF, dh), BF16))
                if dil > 1:
                    pad_ref[pl.ds(base + ATT_HALF, seg), :] = (
                        src_ref[pl.ds(cls, seg, stride=dil), :].astype(BF16))
            if dil > 1:
                qcm_ref[pl.ds(cls * seg, seg), :] = qr_ref[pl.ds(cls, seg, stride=dil), :].astype(BF16)

        def block(u, carry, g=g, dil=dil, seg=seg, n_blk=n_blk, pitch=pitch):
            cls = lax.shift_right_logical(u, n_blk.bit_length() - 1)
            q0 = (u & (n_blk - 1)) * ATT_TQ
            if dil == 1:
                tok_rows = pl.ds(pl.multiple_of(q0, ATT_TQ), ATT_TQ)
            else:
                tok_rows = pl.ds(cls + dil * q0, ATT_TQ, stride=dil)
            qb = qcm_ref[pl.ds(pl.multiple_of(u * ATT_TQ, ATT_TQ), ATT_TQ), :]
            win = pl.ds(pl.multiple_of(cls * pitch + q0, ATT_TQ), ATT_TK)
            s = _dot_nt(qb, kcm_ref[win, :]) * scale
            kj = kj_rel + q0
            valid = band & (kj >= 0) & (kj < seg)
            s = jnp.where(valid, s, NEG_INF)
            m = jnp.max(s, axis=-1, keepdims=True)
            p = jnp.exp(s - m)
            denom = jnp.sum(p, axis=-1, keepdims=True)
            o = _dot(p.astype(BF16), vcm_ref[win, :]) / denom
            lse = m + jnp.log(denom)
            og_ref[g, tok_rows, :] = o
            lse_ref[g, tok_rows, :] = jnp.broadcast_to(lse, (ATT_TQ, dh))
            return carry

        lax.fori_loop(0, dil * n_blk, block, 0, unroll=16)

    def combine(s, carry):
        rows = pl.ds(pl.multiple_of(s * ATT_TK, ATT_TK), ATT_TK)
        lses = [lse_ref[g, rows, :] for g in range(n_g)]
        mx = functools.reduce(jnp.maximum, lses)
        ws = [jnp.exp(l - mx) for l in lses]
        wsum = functools.reduce(lambda a, b: a + b, ws)
        acc = ws[0] * og_ref[0, rows, :]
        for g in range(1, n_g):
            acc = acc + ws[g] * og_ref[g, rows, :]
        out_ref[rows, :] = (acc / wsum).astype(out_ref.dtype)
        return carry

    lax.fori_loop(0, seq // ATT_TK, combine, 0, unroll=2)


def _rope_tables(seq):
    inv_freq = (ROPE_THETA ** (-np.arange(0, ROPE_DIM, 2, dtype=np.float32) / ROPE_DIM)).astype(np.float32)
    ang = (np.arange(seq, dtype=np.float32)[:, None] * inv_freq).astype(np.float32).astype(np.float64)
    cos, sin = np.cos(ang), np.sin(ang)
    pad1 = np.ones((seq, HEAD_DIM - ROPE_DIM))
    pad0 = np.zeros((seq, HEAD_DIM - ROPE_DIM))
    return (jnp.asarray(np.concatenate([cos, cos, pad1], axis=1), F32),
            jnp.asarray(np.concatenate([-sin, sin, pad0], axis=1), F32))


def attn(rest, *, col_qkv):
    bsz, seq, _ = rest.shape
    dh = HEAD_DIM
    hpg = ATTN_HEADS_PER_GROUP
    n_g = len(ATTN_GROUPS)
    assert seq % (max(d for _, d in ATTN_GROUPS) * ATT_TQ) == 0
    cosf, sinf = _rope_tables(seq)

    def col(g, t):
        base = col_qkv // dh + (g * 3 + t) * hpg
        return lambda b, j: (b, 0, base + j)

    in_specs = [pl.BlockSpec((None, seq, dh), col(g, t)) for g in range(n_g) for t in range(3)]
    in_specs += [pl.BlockSpec((seq, dh), lambda b, j: (0, 0)) for _ in range(2)]
    pad_rows = max(dil * (seq // dil + ATT_TK) for _, dil in ATTN_GROUPS)
    return pl.pallas_call(
        functools.partial(_attn_kernel, seq=seq),
        grid=(bsz, hpg),
        in_specs=in_specs,
        out_specs=pl.BlockSpec((None, seq, dh), lambda b, j: (b, 0, j)),
        out_shape=jax.ShapeDtypeStruct((bsz, seq, hpg * dh), BF16),
        scratch_shapes=[
            pltpu.VMEM((seq, dh), F32), pltpu.VMEM((seq, dh), F32), pltpu.VMEM((seq, dh), F32),
            pltpu.VMEM((seq, dh), BF16),
            pltpu.VMEM((pad_rows, dh), BF16), pltpu.VMEM((pad_rows, dh), BF16),
            pltpu.VMEM((n_g, seq, dh), F32), pltpu.VMEM((n_g, seq, dh), F32),
        ],
        compiler_params=_params(("parallel", "parallel")),
        name="attn",
    )(*([rest] * (3 * n_g)), cosf, sinf)


def _mix_out_kernel(oa_ref, ob_ref, ga_ref, gb_ref, h_ref, wa_ref, wb_ref, wo_ref, g_ref, b_ref,
                    out_ref, *, alpha):
    ya = _dot(oa_ref[...], wa_ref[...])
    yb = _dot(ob_ref[...], wb_ref[...])
    z = _sigmoid(ga_ref[...].astype(F32)) * ya + _sigmoid(gb_ref[...].astype(F32)) * yb
    mix = _dot(z.astype(BF16), wo_ref[...])
    out_ref[...] = _layer_norm_rows(alpha * h_ref[...] + mix, g_ref[...], b_ref[...])


def mix_out(oa, ob, rest2d, h, wa_bf, wb_bf, wo_bf, g, b, *, alpha, col_gate, tm=256):
    m, d = h.shape
    wa_w = oa.shape[1]
    wb_w = ob.shape[1]
    assert m % tm == 0 and col_gate % d == 0
    const = lambda i: (0, 0)
    single = pl.Buffered(1)
    return pl.pallas_call(
        functools.partial(_mix_out_kernel, alpha=alpha),
        grid=(m // tm,),
        in_specs=[
            pl.BlockSpec((tm, wa_w), lambda i: (i, 0)),
            pl.BlockSpec((tm, wb_w), lambda i: (i, 0)),
            pl.BlockSpec((tm, d), lambda i: (i, col_gate // d)),
            pl.BlockSpec((tm, d), lambda i: (i, col_gate // d + 1)),
            pl.BlockSpec((tm, d), lambda i: (i, 0)),
            pl.BlockSpec((wa_w, d), const, pipeline_mode=single),
            pl.BlockSpec((wb_w, d), const, pipeline_mode=single),
            pl.BlockSpec((d, d), const, pipeline_mode=single),
            pl.BlockSpec((1, d), const),
            pl.BlockSpec((1, d), const),
        ],
        out_specs=pl.BlockSpec((tm, d), lambda i: (i, 0)),
        out_shape=jax.ShapeDtypeStruct((m, d), F32),
        compiler_params=_params(("parallel",)),
        name="mix_out",
    )(oa, ob, rest2d, rest2d, h, wa_bf, wb_bf, wo_bf, g.reshape(1, d), b.reshape(1, d))


def kernel(x, ffn1_w_in, ffn1_w_out, ln1_g, ln1_b, mix_w_in, hgrn_lb_fwd, hgrn_lb_bwd, hgrn_norm_g,
           w_branch_a, w_branch_b, mix_w_out, ln2_g, ln2_b, ffn2_w_in, ffn2_w_out, ln3_g, ln3_b):
    bsz, seq, d = x.shape
    depth = ffn1_w_in.shape[0]
    alpha = (2.0 * depth) ** 0.25
    hw = w_branch_a.shape[1]
    n_heads = hw // HGRN_HEAD_DIM
    qkv_w = len(ATTN_GROUPS) * 3 * ATTN_HEADS_PER_GROUP * HEAD_DIM
    m = bsz * seq

    h = x.reshape(m, d)
    for layer in range(depth):
        h, h_bf = ffn_ln(h, ffn1_w_in[layer].astype(BF16), ffn1_w_out[layer].astype(BF16),
                         ln1_g[layer], ln1_b[layer], alpha=alpha, emit_bf16=True)

        w_in = mix_w_in[layer]
        w_hf = w_in[:, hw:3 * hw].astype(BF16)
        w_rest = jnp.concatenate([w_in[:, :hw], w_in[:, 3 * hw:5 * hw], w_in[:, 5 * hw + qkv_w:],
                                  w_in[:, 5 * hw:5 * hw + qkv_w]], axis=1).astype(BF16)
        hf = matmul(h_bf, w_hf, F32).reshape(bsz, seq, 2 * hw)
        rest2d = matmul(h_bf, w_rest, BF16, tm=1024, tn=w_rest.shape[1] // 4)
        rest = rest2d.reshape(bsz, seq, -1)
        col_hq, col_hi, col_hog, col_gate = 0, hw, 2 * hw, 3 * hw
        col_qkv = 3 * hw + 2 * d

        o_a = hgrn(rest, hf, hgrn_lb_fwd, hgrn_lb_bwd, hgrn_norm_g[layer], layer=layer,
                   n_heads=n_heads, col_hq=col_hq, col_hi=col_hi, col_hog=col_hog)
        o_b = attn(rest, col_qkv=col_qkv)

        h = mix_out(o_a.reshape(m, hw), o_b.reshape(m, -1), rest2d, h,
                    w_branch_a[layer].astype(BF16), w_branch_b[layer].astype(BF16),
                    mix_w_out[layer].astype(BF16), ln2_g[layer], ln2_b[layer],
                    alpha=alpha, col_gate=col_gate)

        (h,) = ffn_ln(h, ffn2_w_in[layer].astype(BF16), ffn2_w_out[layer].astype(BF16),
                      ln3_g[layer], ln3_b[layer], alpha=alpha, emit_bf16=False)
    return h.reshape(bsz, seq, d)
```

```python
import functools

import jax
import jax.numpy as jnp
from jax import lax
from jax.experimental import pallas as pl
from jax.experimental.pallas import tpu as pltpu

F32 = jnp.float32
BF16 = jnp.bfloat16

HGRN_HEAD_DIM = 128
HGRN_CHUNK = 32
HGRN_SLAB = 4 * HGRN_CHUNK
HGRN_UNROLL = 16
ATTN_GROUPS = ((128, 1), (512, 4), (2048, 16))
ATTN_HEADS_PER_GROUP = 4
HEAD_DIM = 128
ROPE_THETA = 500000.0
ROPE_DIM = HEAD_DIM // 4
LN_EPS = 1e-5
NEG_INF = -1e30

V7X_VMEM_LIMIT_BYTES = 60000 * 1024


def _params(semantics):
    return pltpu.CompilerParams(dimension_semantics=semantics,
                                vmem_limit_bytes=V7X_VMEM_LIMIT_BYTES)


def _dot(a, b):
    return jnp.dot(a, b, preferred_element_type=F32)


def _dot_nt(a, b):
    return lax.dot_general(a, b, (((1,), (1,)), ((), ())), preferred_element_type=F32)


def _dot_tn(a, b):
    return lax.dot_general(a, b, (((0,), (0,)), ((), ())), preferred_element_type=F32)


def _sigmoid(x):
    return 1.0 / (1.0 + jnp.exp(-x))


def _layer_norm_rows(y, g, b):
    mu = jnp.mean(y, axis=-1, keepdims=True)
    d = y - mu
    var = jnp.mean(d * d, axis=-1, keepdims=True)
    return d * lax.rsqrt(var + LN_EPS) * g + b


def _ffn_ln_kernel(x_ref, wg_ref, wu_ref, wo_ref, g_ref, b_ref, *rest, alpha, n_f):
    (out_ref, *maybe_outbf), (acc_ref, xbf_ref) = rest[:-2], rest[-2:]
    f = pl.program_id(1)

    @pl.when(f == 0)
    def _():
        acc_ref[...] = jnp.zeros_like(acc_ref)
        xbf_ref[...] = x_ref[...].astype(BF16)

    xb = xbf_ref[...]
    gate = _dot(xb, wg_ref[...])
    up = _dot(xb, wu_ref[...])
    hid = (gate * _sigmoid(gate)) * up
    acc_ref[...] += _dot(hid.astype(BF16), wo_ref[...])

    @pl.when(f == n_f - 1)
    def _():
        y = alpha * x_ref[...] + 0.5 * acc_ref[...]
        h = _layer_norm_rows(y, g_ref[...], b_ref[...])
        out_ref[...] = h
        for outbf_ref in maybe_outbf:
            outbf_ref[...] = h.astype(BF16)


def ffn_ln(x, w_in_bf, w_out_bf, g, b, *, alpha, emit_bf16, tm=512, tf=512):
    m, d = x.shape
    ff = w_out_bf.shape[0]
    assert m % tm == 0 and ff % tf == 0
    n_f = ff // tf
    kern = functools.partial(_ffn_ln_kernel, alpha=alpha, n_f=n_f)
    out_dtypes = [F32, BF16] if emit_bf16 else [F32]
    return pl.pallas_call(
        kern,
        grid=(m // tm, n_f),
        in_specs=[
            pl.BlockSpec((tm, d), lambda i, f: (i, 0)),
            pl.BlockSpec((d, tf), lambda i, f: (0, f)),
            pl.BlockSpec((d, tf), lambda i, f: (0, f + n_f)),
            pl.BlockSpec((tf, d), lambda i, f: (f, 0)),
            pl.BlockSpec((1, d), lambda i, f: (0, 0)),
            pl.BlockSpec((1, d), lambda i, f: (0, 0)),
        ],
        out_specs=[pl.BlockSpec((tm, d), lambda i, f: (i, 0)) for _ in out_dtypes],
        out_shape=[jax.ShapeDtypeStruct((m, d), dt) for dt in out_dtypes],
        scratch_shapes=[pltpu.VMEM((tm, d), F32), pltpu.VMEM((tm, d), BF16)],
        compiler_params=_params(("parallel", "arbitrary")),
        name="ffn_ln",
    )(x, w_in_bf, w_in_bf, w_out_bf, g.reshape(1, d), b.reshape(1, d))


def _matmul_kernel(x_ref, w_ref, o_ref):
    o_ref[...] = _dot(x_ref[...], w_ref[...]).astype(o_ref.dtype)


def matmul(x_bf, w_bf, out_dtype, *, tm=2048, tn=512):
    m, k = x_bf.shape
    n = w_bf.shape[1]
    assert m % tm == 0 and n % tn == 0
    return pl.pallas_call(
        _matmul_kernel,
        grid=(m // tm, n // tn),
        in_specs=[pl.BlockSpec((tm, k), lambda i, j: (i, 0)),
                  pl.BlockSpec((k, tn), lambda i, j: (0, j))],
        out_specs=pl.BlockSpec((tm, tn), lambda i, j: (i, j)),
        out_shape=jax.ShapeDtypeStruct((m, n), out_dtype),
        compiler_params=_params(("parallel", "arbitrary")),
        name="in_proj",
    )(x_bf, w_bf)


def _hgrn_kernel(hq_ref, hff_ref, hfb_ref, hip_ref, lbf_ref, lbb_ref, hio_ref, hog_ref, ng_ref, out_ref,
                 *scratch, seq, layer):
    c = HGRN_CHUNK
    slab = HGRN_SLAB
    n_slab = seq // slab
    dh = HGRN_HEAD_DIM
    assert slab == 4 * c
    t = pl.program_id(0)
    n_set = (len(scratch) - 2) // 2
    sets = (scratch[:n_set], scratch[n_set:2 * n_set])
    dec_ref, kv_ref = scratch[2 * n_set:]

    @pl.when(t == 0)
    def _():
        for ref in sets[1]:
            ref[...] = jnp.zeros(ref.shape, ref.dtype)

    def lower_bound(lb_ref):
        v = lb_ref[...].astype(F32)
        e = jnp.exp(v - jnp.max(v, axis=0, keepdims=True))
        sm = e / jnp.sum(e, axis=0, keepdims=True)
        return jnp.sum(sm[: layer + 1], axis=0, keepdims=True)

    lbs = (lower_bound(lbf_ref), lower_bound(lbb_ref))
    hf_refs = (hff_ref, hfb_ref)

    ri = lax.broadcasted_iota(jnp.int32, (slab, slab), 0)
    ci = lax.broadcasted_iota(jnp.int32, (slab, slab), 1)
    same_chunk = (ri // c) == (ci // c)
    same_pair = (ri // (2 * c)) == (ci // (2 * c))
    lower = ci <= ri
    upper = ci >= ri
    row_in_chunk = lax.broadcasted_iota(jnp.int32, (slab, dh), 0) % c

    def prep_slab(s, wset):
        qd_ref, qmix_ref, q64_ref, q128_ref, kdt_ref, kmixt_ref, k64t_ref, _ = wset
        r0 = pl.multiple_of(s * slab, slab)
        rows = pl.ds(r0, slab)
        q = hq_ref[rows, :].astype(F32)
        q = q * _sigmoid(q)
        qd, ke, e = [], [], []
        for d in range(2):
            f = lbs[d] + (1.0 - lbs[d]) * _sigmoid(hf_refs[d][rows, :].astype(F32))
            cum = jnp.log(f)
            for sh in (1, 2, 4, 8, 16):
                if d == 0:
                    cum = cum + jnp.where(row_in_chunk >= sh, pltpu.roll(cum, sh, 0), 0.0)
                else:
                    cum = cum + jnp.where(row_in_chunk < c - sh, pltpu.roll(cum, slab - sh, 0), 0.0)
            tot_row = c - 1 if d == 0 else 0
            e.append([jnp.exp(cum[j * c + tot_row: j * c + tot_row + 1, :]) for j in range(4)])
            ecum = jnp.exp(cum)
            qd_d = q * ecum
            kd_d = (1.0 - f) / ecum
            qd_ref[d, rows, :] = qd_d.astype(BF16)
            kdt_ref[d, s] = kd_d.T.astype(BF16)
            qd.append([qd_d[j * c:(j + 1) * c] for j in range(4)])
            ke.append([kd_d[j * c:(j + 1) * c] * e[d][j] for j in range(4)])

        cat = lambda pieces: jnp.concatenate(pieces, axis=0)
        qmix_ref[rows, :] = cat([qd[1][0], qd[0][1], qd[1][2], qd[0][3]]).astype(BF16)
        kmixt_ref[s] = cat([ke[0][0], ke[1][1], ke[0][2], ke[1][3]]).T.astype(BF16)
        q64_ref[rows, :] = cat([qd[1][0] * e[1][1], qd[1][1], qd[0][2], qd[0][3] * e[0][2]]).astype(BF16)
        k64t_ref[s] = cat([ke[0][0] * e[0][1], ke[0][1], ke[1][2], ke[1][3] * e[1][2]]).T.astype(BF16)
        k128 = []
        for d in range(2):
            e0, e1, e2, e3 = e[d]
            from_start = (None, e0, e0 * e1, e0 * e1 * e2)
            to_end = (e1 * e2 * e3, e2 * e3, e3, None)
            qmul, kmul = (from_start, to_end) if d == 0 else (to_end, from_start)
            mul = lambda x, m: x if m is None else x * m
            q128_ref[rows, d * dh:(d + 1) * dh] = cat([mul(qd[d][j], qmul[j]) for j in range(4)]).astype(BF16)
            k128.append(cat([mul(ke[d][j], kmul[j]) for j in range(4)]).astype(BF16))
            dec_ref[d, pl.ds(pl.multiple_of(s * 8, 8), 8), :] = jnp.broadcast_to(e0 * e1 * e2 * e3, (8, dh))
        kv_ref[s] = _dot_tn(hip_ref[rows, :], jnp.concatenate(k128, axis=1))

    def state_step(i, st, d, snap_ref):
        s = i if d == 0 else n_slab - 1 - i
        snap_ref[s, :, d * dh:(d + 1) * dh] = st
        return st * dec_ref[d, pl.ds(s * 8, 1), :] + kv_ref[s, :, d * dh:(d + 1) * dh]

    def out_slab(s, rset):
        qd_ref, qmix_ref, q64_ref, q128_ref, kdt_ref, kmixt_ref, k64t_ref, snap_ref = rset
        rows = pl.ds(pl.multiple_of(s * slab, slab), slab)
        m_df = _dot(qd_ref[0, rows, :], kdt_ref[0, s])
        m_db = _dot(qd_ref[1, rows, :], kdt_ref[1, s])
        m_32 = _dot(qmix_ref[rows, :], kmixt_ref[s])
        m_64 = _dot(q64_ref[rows, :], k64t_ref[s])
        a = jnp.where(same_chunk,
                      jnp.where(lower, m_df, 0.0) + jnp.where(upper, m_db, 0.0),
                      jnp.where(same_pair, m_32, m_64))
        states = jnp.concatenate([snap_ref[s, :, :dh].T, snap_ref[s, :, dh:].T], axis=0)
        o = _dot(a.astype(BF16), hio_ref[rows, :]) + _dot(q128_ref[rows, :], states.astype(BF16))
        o = o * lax.rsqrt(jnp.mean(o * o, axis=-1, keepdims=True) + LN_EPS)
        og = hog_ref[rows, :].astype(F32)
        out_ref[rows, :] = (o * ng_ref[...] * (og * _sigmoid(og))).astype(out_ref.dtype)

    def run(wset, rset):
        def both(s, carry):
            prep_slab(s, wset)
            out_slab(s, rset)
            return carry

        lax.fori_loop(0, n_slab, both, 0, unroll=HGRN_UNROLL)
        for d in range(2):
            lax.fori_loop(0, n_slab, functools.partial(state_step, d=d, snap_ref=wset[-1]),
                          jnp.zeros((dh, dh), F32))

    for parity in range(2):
        pl.when(lax.rem(t, 2) == parity)(functools.partial(run, sets[parity], sets[1 - parity]))


def hgrn(rest, hf, lb_fwd, lb_bwd, norm_g, *, layer, n_heads, col_hq, col_hi, col_hog):
    bsz, seq, _ = rest.shape
    dh = HGRN_HEAD_DIM
    width = n_heads * dh
    nl = lb_fwd.shape[0]
    slab = HGRN_SLAB
    assert seq % slab == 0
    n_slab = seq // slab
    n_items = bsz * n_heads
    kern = functools.partial(_hgrn_kernel, seq=seq, layer=layer)

    def prep_item(t):
        return jnp.minimum(t, n_items - 1)

    def out_item(t):
        return jnp.maximum(t - 1, 0)

    def seq_cols(item_of, base):
        return lambda t: (item_of(t) // n_heads, 0, base // dh + item_of(t) % n_heads)

    def head_cols(item_of):
        return lambda t: (0, item_of(t) % n_heads)

    return pl.pallas_call(
        kern,
        grid=(n_items + 1,),
        in_specs=[
            pl.BlockSpec((None, seq, dh), seq_cols(prep_item, col_hq)),
            pl.BlockSpec((None, seq, dh), seq_cols(prep_item, 0)),
            pl.BlockSpec((None, seq, dh), seq_cols(prep_item, width)),
            pl.BlockSpec((None, seq, dh), seq_cols(prep_item, col_hi)),
            pl.BlockSpec((nl, dh), head_cols(prep_item)),
            pl.BlockSpec((nl, dh), head_cols(prep_item)),
            pl.BlockSpec((None, seq, dh), seq_cols(out_item, col_hi)),
            pl.BlockSpec((None, seq, dh), seq_cols(out_item, col_hog)),
            pl.BlockSpec((1, dh), head_cols(out_item)),
        ],
        out_specs=pl.BlockSpec((None, seq, dh), seq_cols(out_item, 0)),
        out_shape=jax.ShapeDtypeStruct((bsz, seq, width), BF16),
        scratch_shapes=2 * [
            pltpu.VMEM((2, seq, dh), BF16),
            pltpu.VMEM((seq, dh), BF16),
            pltpu.VMEM((seq, dh), BF16),
            pltpu.VMEM((seq, 2 * dh), BF16),
            pltpu.VMEM((2, n_slab, dh, slab), BF16),
            pltpu.VMEM((n_slab, dh, slab), BF16),
            pltpu.VMEM((n_slab, dh, slab), BF16),
            pltpu.VMEM((n_slab, dh, 2 * dh), F32),
        ] + [
            pltpu.VMEM((2, n_slab * 8, dh), F32),
            pltpu.VMEM((n_slab, dh, 2 * dh), F32),
        ],
        compiler_params=_params(("arbitrary",)),
        name="hgrn",
    )(rest, hf, hf, rest, lb_fwd, lb_bwd, rest, rest, norm_g.reshape(1, width))


ATT_TQ = 128
ATT_HALF = 64
ATT_TK = ATT_TQ + 2 * ATT_HALF


def _attn_kernel(*refs, seq):
    n_g = len(ATTN_GROUPS)
    qkv_refs = refs[: 3 * n_g]
    cos_ref, sin_ref, out_ref = refs[3 * n_g: 3 * n_g + 3]
    qr_ref, kr_ref, vf_ref, qcm_ref, kcm_ref, vcm_ref, og_ref, lse_ref = refs[3 * n_g + 3:]
    dh = HEAD_DIM
    scale = HEAD_DIM ** -0.5

    lane = lax.broadcasted_iota(jnp.int32, (ATT_TK, dh), 1)

    def rope(x_ref, rows):
        t = x_ref[rows, :].astype(F32)
        swapped = jnp.where(lane < ROPE_DIM // 2,
                            pltpu.roll(t, dh - ROPE_DIM // 2, 1),
                            pltpu.roll(t, ROPE_DIM // 2, 1))
        return t * cos_ref[rows, :] + swapped * sin_ref[rows, :]

    qi_rel = lax.broadcasted_iota(jnp.int32, (ATT_TQ, ATT_TK), 0)
    kj_rel = lax.broadcasted_iota(jnp.int32, (ATT_TQ, ATT_TK), 1) - ATT_HALF
    band = jnp.abs(qi_rel - kj_rel) <= ATT_HALF

    for g, (window, dil) in enumerate(ATTN_GROUPS):
        assert window // (2 * dil) == ATT_HALF
        seg = seq // dil
        n_blk = seg // ATT_TQ
        assert n_blk & (n_blk - 1) == 0
        pitch = seg + ATT_TK
        q_ref, k_ref, v_ref = qkv_refs[3 * g: 3 * g + 3]

        def stage(s, carry, dil=dil, q_ref=q_ref, k_ref=k_ref, v_ref=v_ref):
            rows = pl.ds(pl.multiple_of(s * ATT_TK, ATT_TK), ATT_TK)
            q = rope(q_ref, rows)
            k = rope(k_ref, rows)
            if dil == 1:
                pad_rows = pl.ds(pl.multiple_of(s * ATT_TK + ATT_HALF, ATT_HALF), ATT_TK)
                qcm_ref[rows, :] = q.astype(BF16)
                kcm_ref[pad_rows, :] = k.astype(BF16)
                vcm_ref[pad_rows, :] = v_ref[rows, :]
            else:
                qr_ref[rows, :] = q
                kr_ref[rows, :] = k
                vf_ref[rows, :] = v_ref[rows, :].astype(F32)
            return carry

        lax.fori_loop(0, seq // ATT_TK, stage, 0, unroll=4)
        for cls in range(dil):
            base = cls * pitch
            for pad_ref, src_ref in ((kcm_ref, kr_ref), (vcm_ref, vf_ref)):
                pad_ref[pl.ds(base, ATT_HALF), :] = jnp.zeros((ATT_HALF, dh), BF16)
                pad_ref[pl.ds(base + ATT_HALF + seg, ATT_TK - ATT_HALF), :] = (
                    jnp.zeros((ATT_TK - ATT_HALF, dh), BF16))
                if dil > 1:
                    pad_ref[pl.ds(base + ATT_HALF, seg), :] = (
                        src_ref[pl.ds(cls, seg, stride=dil), :].astype(BF16))
            if dil > 1:
                qcm_ref[pl.ds(cls * seg, seg), :] = qr_ref[pl.ds(cls, seg, stride=dil), :].astype(BF16)

        def block(u, carry, g=g, dil=dil, seg=seg, n_blk=n_blk, pitch=pitch):
            cls = lax.shift_right_logical(u, n_blk.bit_length() - 1)
            q0 = (u & (n_blk - 1)) * ATT_TQ
            if dil == 1:
                tok_rows = pl.ds(pl.multiple_of(q0, ATT_TQ), ATT_TQ)
            else:
                tok_rows = pl.ds(cls + dil * q0, ATT_TQ, stride=dil)
            qb = qcm_ref[pl.ds(pl.multiple_of(u * ATT_TQ, ATT_TQ), ATT_TQ), :]
            win = pl.ds(pl.multiple_of(cls * pitch + q0, ATT_TQ), ATT_TK)
            s = _dot_nt(qb, kcm_ref[win, :]) * scale
            kj = kj_rel + q0
            valid = band & (kj >= 0) & (kj < seg)
            s = jnp.where(valid, s, NEG_INF)
            m = jnp.max(s, axis=-1, keepdims=True)
            p = jnp.exp(s - m)
            denom = jnp.sum(p, axis=-1, keepdims=True)
            o = _dot(p.astype(BF16), vcm_ref[win, :]) / denom
            lse = m + jnp.log(denom)
            og_ref[g, tok_rows, :] = o
            lse_ref[g, tok_rows, :] = jnp.broadcast_to(lse, (ATT_TQ, dh))
            return carry

        lax.fori_loop(0, dil * n_blk, block, 0, unroll=16)

    def combine(s, carry):
        rows = pl.ds(pl.multiple_of(s * ATT_TK, ATT_TK), ATT_TK)
        lses = [lse_ref[g, rows, :] for g in range(n_g)]
        mx = functools.reduce(jnp.maximum, lses)
        ws = [jnp.exp(l - mx) for l in lses]
        wsum = functools.reduce(lambda a, b: a + b, ws)
        acc = ws[0] * og_ref[0, rows, :]
        for g in range(1, n_g):
            acc = acc + ws[g] * og_ref[g, rows, :]
        out_ref[rows, :] = (acc / wsum).astype(out_ref.dtype)
        return carry

    lax.fori_loop(0, seq // ATT_TK, combine, 0, unroll=2)


def _rope_tables(seq):
    inv_freq = ROPE_THETA ** (-jnp.arange(0, ROPE_DIM, 2, dtype=F32) / ROPE_DIM)
    ang = jnp.arange(seq).astype(F32)[:, None] * inv_freq
    cos, sin = jnp.cos(ang), jnp.sin(ang)
    pad1 = jnp.ones((seq, HEAD_DIM - ROPE_DIM), F32)
    pad0 = jnp.zeros((seq, HEAD_DIM - ROPE_DIM), F32)
    return (jnp.concatenate([cos, cos, pad1], axis=1),
            jnp.concatenate([-sin, sin, pad0], axis=1))


def attn(rest, *, col_qkv):
    bsz, seq, _ = rest.shape
    dh = HEAD_DIM
    hpg = ATTN_HEADS_PER_GROUP
    n_g = len(ATTN_GROUPS)
    assert seq % (max(d for _, d in ATTN_GROUPS) * ATT_TQ) == 0
    cosf, sinf = _rope_tables(seq)

    def col(g, t):
        base = col_qkv // dh + (g * 3 + t) * hpg
        return lambda b, j: (b, 0, base + j)

    in_specs = [pl.BlockSpec((None, seq, dh), col(g, t)) for g in range(n_g) for t in range(3)]
    in_specs += [pl.BlockSpec((seq, dh), lambda b, j: (0, 0)) for _ in range(2)]
    pad_rows = max(dil * (seq // dil + ATT_TK) for _, dil in ATTN_GROUPS)
    return pl.pallas_call(
        functools.partial(_attn_kernel, seq=seq),
        grid=(bsz, hpg),
        in_specs=in_specs,
        out_specs=pl.BlockSpec((None, seq, dh), lambda b, j: (b, 0, j)),
        out_shape=jax.ShapeDtypeStruct((bsz, seq, hpg * dh), BF16),
        scratch_shapes=[
            pltpu.VMEM((seq, dh), F32), pltpu.VMEM((seq, dh), F32), pltpu.VMEM((seq, dh), F32),
            pltpu.VMEM((seq, dh), BF16),
            pltpu.VMEM((pad_rows, dh), BF16), pltpu.VMEM((pad_rows, dh), BF16),
            pltpu.VMEM((n_g, seq, dh), F32), pltpu.VMEM((n_g, seq, dh), F32),
        ],
        compiler_params=_params(("parallel", "parallel")),
        name="attn",
    )(*([rest] * (3 * n_g)), cosf, sinf)


def _mix_out_kernel(oa_ref, ob_ref, ga_ref, gb_ref, h_ref, wa_ref, wb_ref, wo_ref, g_ref, b_ref,
                    out_ref, *, alpha):
    ya = _dot(oa_ref[...], wa_ref[...])
    yb = _dot(ob_ref[...], wb_ref[...])
    z = _sigmoid(ga_ref[...].astype(F32)) * ya + _sigmoid(gb_ref[...].astype(F32)) * yb
    mix = _dot(z.astype(BF16), wo_ref[...])
    out_ref[...] = _layer_norm_rows(alpha * h_ref[...] + mix, g_ref[...], b_ref[...])


def mix_out(oa, ob, rest2d, h, wa_bf, wb_bf, wo_bf, g, b, *, alpha, col_gate, tm=256):
    m, d = h.shape
    wa_w = oa.shape[1]
    wb_w = ob.shape[1]
    assert m % tm == 0 and col_gate % d == 0
    const = lambda i: (0, 0)
    single = pl.Buffered(1)
    return pl.pallas_call(
        functools.partial(_mix_out_kernel, alpha=alpha),
        grid=(m // tm,),
        in_specs=[
            pl.BlockSpec((tm, wa_w), lambda i: (i, 0)),
            pl.BlockSpec((tm, wb_w), lambda i: (i, 0)),
            pl.BlockSpec((tm, d), lambda i: (i, col_gate // d)),
            pl.BlockSpec((tm, d), lambda i: (i, col_gate // d + 1)),
            pl.BlockSpec((tm, d), lambda i: (i, 0)),
            pl.BlockSpec((wa_w, d), const, pipeline_mode=single),
            pl.BlockSpec((wb_w, d), const, pipeline_mode=single),
            pl.BlockSpec((d, d), const, pipeline_mode=single),
            pl.BlockSpec((1, d), const),
            pl.BlockSpec((1, d), const),
        ],
        out_specs=pl.BlockSpec((tm, d), lambda i: (i, 0)),
        out_shape=jax.ShapeDtypeStruct((m, d), F32),
        compiler_params=_params(("parallel",)),
        name="mix_out",
    )(oa, ob, rest2d, rest2d, h, wa_bf, wb_bf, wo_bf, g.reshape(1, d), b.reshape(1, d))


def kernel(x, ffn1_w_in, ffn1_w_out, ln1_g, ln1_b, mix_w_in, hgrn_lb_fwd, hgrn_lb_bwd, hgrn_norm_g,
           w_branch_a, w_branch_b, mix_w_out, ln2_g, ln2_b, ffn2_w_in, ffn2_w_out, ln3_g, ln3_b):
    bsz, seq, d = x.shape
    depth = ffn1_w_in.shape[0]
    alpha = (2.0 * depth) ** 0.25
    hw = w_branch_a.shape[1]
    n_heads = hw // HGRN_HEAD_DIM
    qkv_w = len(ATTN_GROUPS) * 3 * ATTN_HEADS_PER_GROUP * HEAD_DIM
    m = bsz * seq

    h = x.reshape(m, d)
    for layer in range(depth):
        h, h_bf = ffn_ln(h, ffn1_w_in[layer].astype(BF16), ffn1_w_out[layer].astype(BF16),
                         ln1_g[layer], ln1_b[layer], alpha=alpha, emit_bf16=True)

        w_in = mix_w_in[layer]
        w_hf = w_in[:, hw:3 * hw].astype(BF16)
        w_rest = jnp.concatenate([w_in[:, :hw], w_in[:, 3 * hw:5 * hw], w_in[:, 5 * hw + qkv_w:],
                                  w_in[:, 5 * hw:5 * hw + qkv_w]], axis=1).astype(BF16)
        hf = matmul(h_bf, w_hf, F32).reshape(bsz, seq, 2 * hw)
        rest2d = matmul(h_bf, w_rest, BF16, tm=1024, tn=w_rest.shape[1] // 4)
        rest = rest2d.reshape(bsz, seq, -1)
        col_hq, col_hi, col_hog, col_gate = 0, hw, 2 * hw, 3 * hw
        col_qkv = 3 * hw + 2 * d

        o_a = hgrn(rest, hf, hgrn_lb_fwd, hgrn_lb_bwd, hgrn_norm_g[layer], layer=layer,
                   n_heads=n_heads, col_hq=col_hq, col_hi=col_hi, col_hog=col_hog)
        o_b = attn(rest, col_qkv=col_qkv)

        h = mix_out(o_a.reshape(m, hw), o_b.reshape(m, -1), rest2d, h,
                    w_branch_a[layer].astype(BF16), w_branch_b[layer].astype(BF16),
                    mix_w_out[layer].astype(BF16), ln2_g[layer], ln2_b[layer],
                    alpha=alpha, col_gate=col_gate)

        (h,) = ffn_ln(h, ffn2_w_in[layer].astype(BF16), ffn2_w_out[layer].astype(BF16),
                      ln3_g[layer], ln3_b[layer], alpha=alpha, emit_bf16=False)
    return h.reshape(bsz, seq, d)
```

```python
import functools

import jax
import jax.numpy as jnp
from jax import lax
from jax.experimental import pallas as pl
from jax.experimental.pallas import tpu as pltpu

F32 = jnp.float32
BF16 = jnp.bfloat16

HGRN_HEAD_DIM = 128
HGRN_CHUNK = 32
HGRN_SLAB = 4 * HGRN_CHUNK
HGRN_UNROLL = 16
ATTN_GROUPS = ((128, 1), (512, 4), (2048, 16))
ATTN_HEADS_PER_GROUP = 4
HEAD_DIM = 128
ROPE_THETA = 500000.0
ROPE_DIM = HEAD_DIM // 4
LN_EPS = 1e-5
NEG_INF = -1e30

V7X_VMEM_LIMIT_BYTES = 60000 * 1024


def _params(semantics):
    return pltpu.CompilerParams(dimension_semantics=semantics,
                                vmem_limit_bytes=V7X_VMEM_LIMIT_BYTES)


def _dot(a, b):
    return jnp.dot(a, b, preferred_element_type=F32)


def _dot_nt(a, b):
    return lax.dot_general(a, b, (((1,), (1,)), ((), ())), preferred_element_type=F32)


def _dot_tn(a, b):
    return lax.dot_general(a, b, (((0,), (0,)), ((), ())), preferred_element_type=F32)


def _sigmoid(x):
    return 1.0 / (1.0 + jnp.exp(-x))


def _layer_norm_rows(y, g, b):
    mu = jnp.mean(y, axis=-1, keepdims=True)
    d = y - mu
    var = jnp.mean(d * d, axis=-1, keepdims=True)
    return d * lax.rsqrt(var + LN_EPS) * g + b


def _ffn_ln_kernel(x_ref, wg_ref, wu_ref, wo_ref, g_ref, b_ref, *rest, alpha, n_f):
    (out_ref, *maybe_outbf), (acc_ref, xbf_ref) = rest[:-2], rest[-2:]
    f = pl.program_id(1)

    @pl.when(f == 0)
    def _():
        acc_ref[...] = jnp.zeros_like(acc_ref)
        xbf_ref[...] = x_ref[...].astype(BF16)

    xb = xbf_ref[...]
    gate = _dot(xb, wg_ref[...])
    up = _dot(xb, wu_ref[...])
    hid = (gate * _sigmoid(gate)) * up
    acc_ref[...] += _dot(hid.astype(BF16), wo_ref[...])

    @pl.when(f == n_f - 1)
    def _():
        y = alpha * x_ref[...] + 0.5 * acc_ref[...]
        h = _layer_norm_rows(y, g_ref[...], b_ref[...])
        out_ref[...] = h
        for outbf_ref in maybe_outbf:
            outbf_ref[...] = h.astype(BF16)


def ffn_ln(x, w_in_bf, w_out_bf, g, b, *, alpha, emit_bf16, tm=512, tf=512):
    m, d = x.shape
    ff = w_out_bf.shape[0]
    assert m % tm == 0 and ff % tf == 0
    n_f = ff // tf
    kern = functools.partial(_ffn_ln_kernel, alpha=alpha, n_f=n_f)
    out_dtypes = [F32, BF16] if emit_bf16 else [F32]
    return pl.pallas_call(
        kern,
        grid=(m // tm, n_f),
        in_specs=[
            pl.BlockSpec((tm, d), lambda i, f: (i, 0)),
            pl.BlockSpec((d, tf), lambda i, f: (0, f)),
            pl.BlockSpec((d, tf), lambda i, f: (0, f + n_f)),
            pl.BlockSpec((tf, d), lambda i, f: (f, 0)),
            pl.BlockSpec((1, d), lambda i, f: (0, 0)),
            pl.BlockSpec((1, d), lambda i, f: (0, 0)),
        ],
        out_specs=[pl.BlockSpec((tm, d), lambda i, f: (i, 0)) for _ in out_dtypes],
        out_shape=[jax.ShapeDtypeStruct((m, d), dt) for dt in out_dtypes],
        scratch_shapes=[pltpu.VMEM((tm, d), F32), pltpu.VMEM((tm, d), BF16)],
        compiler_params=_params(("parallel", "arbitrary")),
        name="ffn_ln",
    )(x, w_in_bf, w_in_bf, w_out_bf, g.reshape(1, d), b.reshape(1, d))


def _matmul_kernel(x_ref, w_ref, o_ref):
    o_ref[...] = _dot(x_ref[...], w_ref[...]).astype(o_ref.dtype)


def matmul(x_bf, w_bf, out_dtype, *, tm=2048, tn=512):
    m, k = x_bf.shape
    n = w_bf.shape[1]
    assert m % tm == 0 and n % tn == 0
    return pl.pallas_call(
        _matmul_kernel,
        grid=(m // tm, n // tn),
        in_specs=[pl.BlockSpec((tm, k), lambda i, j: (i, 0)),
                  pl.BlockSpec((k, tn), lambda i, j: (0, j))],
        out_specs=pl.BlockSpec((tm, tn), lambda i, j: (i, j)),
        out_shape=jax.ShapeDtypeStruct((m, n), out_dtype),
        compiler_params=_params(("parallel", "arbitrary")),
        name="in_proj",
    )(x_bf, w_bf)


def _hgrn_kernel(hq_ref, hff_ref, hfb_ref, hip_ref, lbf_ref, lbb_ref, hio_ref, hog_ref, ng_ref, out_ref,
                 *scratch, seq, layer):
    c = HGRN_CHUNK
    slab = HGRN_SLAB
    n_slab = seq // slab
    dh = HGRN_HEAD_DIM
    assert slab == 4 * c
    t = pl.program_id(0)
    n_set = (len(scratch) - 2) // 2
    sets = (scratch[:n_set], scratch[n_set:2 * n_set])
    dec_ref, kv_ref = scratch[2 * n_set:]

    @pl.when(t == 0)
    def _():
        for ref in sets[1]:
            ref[...] = jnp.zeros(ref.shape, ref.dtype)

    def lower_bound(lb_ref):
        v = lb_ref[...].astype(F32)
        e = jnp.exp(v - jnp.max(v, axis=0, keepdims=True))
        sm = e / jnp.sum(e, axis=0, keepdims=True)
        return jnp.sum(sm[: layer + 1], axis=0, keepdims=True)

    lbs = (lower_bound(lbf_ref), lower_bound(lbb_ref))
    hf_refs = (hff_ref, hfb_ref)

    ri = lax.broadcasted_iota(jnp.int32, (slab, slab), 0)
    ci = lax.broadcasted_iota(jnp.int32, (slab, slab), 1)
    same_chunk = (ri // c) == (ci // c)
    same_pair = (ri // (2 * c)) == (ci // (2 * c))
    lower = ci <= ri
    upper = ci >= ri
    row_in_chunk = lax.broadcasted_iota(jnp.int32, (slab, dh), 0) % c

    def prep_slab(s, wset):
        qd_ref, qmix_ref, q64_ref, q128_ref, kdt_ref, kmixt_ref, k64t_ref, _ = wset
        r0 = pl.multiple_of(s * slab, slab)
        rows = pl.ds(r0, slab)
        q = hq_ref[rows, :].astype(F32)
        q = q * _sigmoid(q)
        qd, ke, e = [], [], []
        for d in range(2):
            f = lbs[d] + (1.0 - lbs[d]) * _sigmoid(hf_refs[d][rows, :].astype(F32))
            cum = jnp.log(f)
            for sh in (1, 2, 4, 8, 16):
                if d == 0:
                    cum = cum + jnp.where(row_in_chunk >= sh, pltpu.roll(cum, sh, 0), 0.0)
                else:
                    cum = cum + jnp.where(row_in_chunk < c - sh, pltpu.roll(cum, slab - sh, 0), 0.0)
            tot_row = c - 1 if d == 0 else 0
            e.append([jnp.exp(cum[j * c + tot_row: j * c + tot_row + 1, :]) for j in range(4)])
            ecum = jnp.exp(cum)
            qd_d = q * ecum
            kd_d = (1.0 - f) / ecum
            qd_ref[d, rows, :] = qd_d.astype(BF16)
            kdt_ref[d, s] = kd_d.T.astype(BF16)
            qd.append([qd_d[j * c:(j + 1) * c] for j in range(4)])
            ke.append([kd_d[j * c:(j + 1) * c] * e[d][j] for j in range(4)])

        cat = lambda pieces: jnp.concatenate(pieces, axis=0)
        qmix_ref[rows, :] = cat([qd[1][0], qd[0][1], qd[1][2], qd[0][3]]).astype(BF16)
        kmixt_ref[s] = cat([ke[0][0], ke[1][1], ke[0][2], ke[1][3]]).T.astype(BF16)
        q64_ref[rows, :] = cat([qd[1][0] * e[1][1], qd[1][1], qd[0][2], qd[0][3] * e[0][2]]).astype(BF16)
        k64t_ref[s] = cat([ke[0][0] * e[0][1], ke[0][1], ke[1][2], ke[1][3] * e[1][2]]).T.astype(BF16)
        k128 = []
        for d in range(2):
            e0, e1, e2, e3 = e[d]
            from_start = (None, e0, e0 * e1, e0 * e1 * e2)
            to_end = (e1 * e2 * e3, e2 * e3, e3, None)
            qmul, kmul = (from_start, to_end) if d == 0 else (to_end, from_start)
            mul = lambda x, m: x if m is None else x * m
            q128_ref[rows, d * dh:(d + 1) * dh] = cat([mul(qd[d][j], qmul[j]) for j in range(4)]).astype(BF16)
            k128.append(cat([mul(ke[d][j], kmul[j]) for j in range(4)]).astype(BF16))
            dec_ref[d, pl.ds(pl.multiple_of(s * 8, 8), 8), :] = jnp.broadcast_to(e0 * e1 * e2 * e3, (8, dh))
        kv_ref[s] = _dot_tn(hip_ref[rows, :], jnp.concatenate(k128, axis=1))

    def state_step(i, st, d, snap_ref):
        s = i if d == 0 else n_slab - 1 - i
        snap_ref[s, :, d * dh:(d + 1) * dh] = st
        return st * dec_ref[d, pl.ds(s * 8, 1), :] + kv_ref[s, :, d * dh:(d + 1) * dh]

    def out_slab(s, rset):
        qd_ref, qmix_ref, q64_ref, q128_ref, kdt_ref, kmixt_ref, k64t_ref, snap_ref = rset
        rows = pl.ds(pl.multiple_of(s * slab, slab), slab)
        m_df = _dot(qd_ref[0, rows, :], kdt_ref[0, s])
        m_db = _dot(qd_ref[1, rows, :], kdt_ref[1, s])
        m_32 = _dot(qmix_ref[rows, :], kmixt_ref[s])
        m_64 = _dot(q64_ref[rows, :], k64t_ref[s])
        a = jnp.where(same_chunk,
                      jnp.where(lower, m_df, 0.0) + jnp.where(upper, m_db, 0.0),
                      jnp.where(same_pair, m_32, m_64))
        states = jnp.concatenate([snap_ref[s, :, :dh].T, snap_ref[s, :, dh:].T], axis=0)
        o = _dot(a.astype(BF16), hio_ref[rows, :]) + _dot(q128_ref[rows, :], states.astype(BF16))
        o = o * lax.rsqrt(jnp.mean(o * o, axis=-1, keepdims=True) + LN_EPS)
        og = hog_ref[rows, :].astype(F32)
        out_ref[rows, :] = (o * ng_ref[...] * (og * _sigmoid(og))).astype(out_ref.dtype)

    def run(wset, rset):
        def both(s, carry):
            prep_slab(s, wset)
            out_slab(s, rset)
            return carry

        lax.fori_loop(0, n_slab, both, 0, unroll=HGRN_UNROLL)
        for d in range(2):
            lax.fori_loop(0, n_slab, functools.partial(state_step, d=d, snap_ref=wset[-1]),
                          jnp.zeros((dh, dh), F32))

    for parity in range(2):
        pl.when(lax.rem(t, 2) == parity)(functools.partial(run, sets[parity], sets[1 - parity]))


def hgrn(rest, hf, lb_fwd, lb_bwd, norm_g, *, layer, n_heads, col_hq, col_hi, col_hog):
    bsz, seq, _ = rest.shape
    dh = HGRN_HEAD_DIM
    width = n_heads * dh
    nl = lb_fwd.shape[0]
    slab = HGRN_SLAB
    assert seq % slab == 0
    n_slab = seq // slab
    n_items = bsz * n_heads
    kern = functools.partial(_hgrn_kernel, seq=seq, layer=layer)

    def prep_item(t):
        return jnp.minimum(t, n_items - 1)

    def out_item(t):
        return jnp.maximum(t - 1, 0)

    def seq_cols(item_of, base):
        return lambda t: (item_of(t) // n_heads, 0, base // dh + item_of(t) % n_heads)

    def head_cols(item_of):
        return lambda t: (0, item_of(t) % n_heads)

    return pl.pallas_call(
        kern,
        grid=(n_items + 1,),
        in_specs=[
            pl.BlockSpec((None, seq, dh), seq_cols(prep_item, col_hq)),
            pl.BlockSpec((None, seq, dh), seq_cols(prep_item, 0)),
            pl.BlockSpec((None, seq, dh), seq_cols(prep_item, width)),
            pl.BlockSpec((None, seq, dh), seq_cols(prep_item, col_hi)),
            pl.BlockSpec((nl, dh), head_cols(prep_item)),
            pl.BlockSpec((nl, dh), head_cols(prep_item)),
            pl.BlockSpec((None, seq, dh), seq_cols(out_item, col_hi)),
            pl.BlockSpec((None, seq, dh), seq_cols(out_item, col_hog)),
            pl.BlockSpec((1, dh), head_cols(out_item)),
        ],
        out_specs=pl.BlockSpec((None, seq, dh), seq_cols(out_item, 0)),
        out_shape=jax.ShapeDtypeStruct((bsz, seq, width), BF16),
        scratch_shapes=2 * [
            pltpu.VMEM((2, seq, dh), BF16),
            pltpu.VMEM((seq, dh), BF16),
            pltpu.VMEM((seq, dh), BF16),
            pltpu.VMEM((seq, 2 * dh), BF16),
            pltpu.VMEM((2, n_slab, dh, slab), BF16),
            pltpu.VMEM((n_slab, dh, slab), BF16),
            pltpu.VMEM((n_slab, dh, slab), BF16),
            pltpu.VMEM((n_slab, dh, 2 * dh), F32),
        ] + [
            pltpu.VMEM((2, n_slab * 8, dh), F32),
            pltpu.VMEM((n_slab, dh, 2 * dh), F32),
        ],
        compiler_params=_params(("arbitrary",)),
        name="hgrn",
    )(rest, hf, hf, rest, lb_fwd, lb_bwd, rest, rest, norm_g.reshape(1, width))


ATT_TQ = 128
ATT_HALF = 64
ATT_TK = ATT_TQ + 2 * ATT_HALF


def _attn_kernel(*refs, seq):
    n_g = len(ATTN_GROUPS)
    qkv_refs = refs[: 3 * n_g]
    cos_ref, sin_ref, out_ref = refs[3 * n_g: 3 * n_g + 3]
    qr_ref, kr_ref, vf_ref, qcm_ref, kcm_ref, vcm_ref, og_ref, lse_ref = refs[3 * n_g + 3:]
    dh = HEAD_DIM
    scale = HEAD_DIM ** -0.5

    lane = lax.broadcasted_iota(jnp.int32, (ATT_TK, dh), 1)

    def rope(x_ref, rows):
        t = x_ref[rows, :].astype(F32)
        swapped = jnp.where(lane < ROPE_DIM // 2,
                            pltpu.roll(t, dh - ROPE_DIM // 2, 1),
                            pltpu.roll(t, ROPE_DIM // 2, 1))
        return t * cos_ref[rows, :] + swapped * sin_ref[rows, :]

    qi_rel = lax.broadcasted_iota(jnp.int32, (ATT_TQ, ATT_TK), 0)
    kj_rel = lax.broadcasted_iota(jnp.int32, (ATT_TQ, ATT_TK), 1) - ATT_HALF
    band = jnp.abs(qi_rel - kj_rel) <= ATT_HALF

    for g, (window, dil) in enumerate(ATTN_GROUPS):
        assert window // (2 * dil) == ATT_HALF
        seg = seq // dil
        n_blk = seg // ATT_TQ
        assert n_blk & (n_blk - 1) == 0
        pitch = seg + ATT_TK
        q_ref, k_ref, v_ref = qkv_refs[3 * g: 3 * g + 3]

        def stage(s, carry, dil=dil, q_ref=q_ref, k_ref=k_ref, v_ref=v_ref):
            rows = pl.ds(pl.multiple_of(s * ATT_TK, ATT_TK), ATT_TK)
            q = rope(q_ref, rows)
            k = rope(k_ref, rows)
            if dil == 1:
                pad_rows = pl.ds(pl.multiple_of(s * ATT_TK + ATT_HALF, ATT_HALF), ATT_TK)
                qcm_ref[rows, :] = q.astype(BF16)
                kcm_ref[pad_rows, :] = k.astype(BF16)
                vcm_ref[pad_rows, :] = v_ref[rows, :]
            else:
                qr_ref[rows, :] = q
                kr_ref[rows, :] = k
                vf_ref[rows, :] = v_ref[rows, :].astype(F32)
            return carry

        lax.fori_loop(0, seq // ATT_TK, stage, 0, unroll=4)
        for cls in range(dil):
            base = cls * pitch
            for pad_ref, src_ref in ((kcm_ref, kr_ref), (vcm_ref, vf_ref)):
                pad_ref[pl.ds(base, ATT_HALF), :] = jnp.zeros((ATT_HALF, dh), BF16)
                pad_ref[pl.ds(base + ATT_HALF + seg, ATT_TK - ATT_HALF), :] = (
                    jnp.zeros((ATT_TK - ATT_HALF, dh), BF16))
                if dil > 1:
                    pad_ref[pl.ds(base + ATT_HALF, seg), :] = (
                        src_ref[pl.ds(cls, seg, stride=dil), :].astype(BF16))
            if dil > 1:
                qcm_ref[pl.ds(cls * seg, seg), :] = qr_ref[pl.ds(cls, seg, stride=dil), :].astype(BF16)

        def block(u, carry, g=g, dil=dil, seg=seg, n_blk=n_blk, pitch=pitch):
            cls = lax.shift_right_logical(u, n_blk.bit_length() - 1)
            q0 = (u & (n_blk - 1)) * ATT_TQ
            if dil == 1:
                tok_rows = pl.ds(pl.multiple_of(q0, ATT_TQ), ATT_TQ)
            else:
                tok_rows = pl.ds(cls + dil * q0, ATT_TQ, stride=dil)
            qb = qcm_ref[pl.ds(pl.multiple_of(u * ATT_TQ, ATT_TQ), ATT_TQ), :]
            win = pl.ds(pl.multiple_of(cls * pitch + q0, ATT_TQ), ATT_TK)
            s = _dot_nt(qb, kcm_ref[win, :]) * scale
            kj = kj_rel + q0
            valid = band & (kj >= 0) & (kj < seg)
            s = jnp.where(valid, s, NEG_INF)
            m = jnp.max(s, axis=-1, keepdims=True)
            p = jnp.exp(s - m)
            denom = jnp.sum(p, axis=-1, keepdims=True)
            o = _dot(p.astype(BF16), vcm_ref[win, :]) / denom
            lse = m + jnp.log(denom)
            og_ref[g, tok_rows, :] = o
            lse_ref[g, tok_rows, :] = jnp.broadcast_to(lse, (ATT_TQ, dh))
            return carry

        lax.fori_loop(0, dil * n_blk, block, 0, unroll=16)

    def combine(s, carry):
        rows = pl.ds(pl.multiple_of(s * ATT_TK, ATT_TK), ATT_TK)
        lses = [lse_ref[g, rows, :] for g in range(n_g)]
        mx = functools.reduce(jnp.maximum, lses)
        ws = [jnp.exp(l - mx) for l in lses]
        wsum = functools.reduce(lambda a, b: a + b, ws)
        acc = ws[0] * og_ref[0, rows, :]
        for g in range(1, n_g):
            acc = acc + ws[g] * og_ref[g, rows, :]
        out_ref[rows, :] = (acc / wsum).astype(out_ref.dtype)
        return carry

    lax.fori_loop(0, seq // ATT_TK, combine, 0, unroll=2)


def _rope_tables(seq):
    inv_freq = ROPE_THETA ** (-jnp.arange(0, ROPE_DIM, 2, dtype=F32) / ROPE_DIM)
    ang = jnp.arange(seq).astype(F32)[:, None] * inv_freq
    cos, sin = jnp.cos(ang), jnp.sin(ang)
    pad1 = jnp.ones((seq, HEAD_DIM - ROPE_DIM), F32)
    pad0 = jnp.zeros((seq, HEAD_DIM - ROPE_DIM), F32)
    return (jnp.concatenate([cos, cos, pad1], axis=1),
            jnp.concatenate([-sin, sin, pad0], axis=1))


def attn(rest, *, col_qkv):
    bsz, seq, _ = rest.shape
    dh = HEAD_DIM
    hpg = ATTN_HEADS_PER_GROUP
    n_g = len(ATTN_GROUPS)
    assert seq % (max(d for _, d in ATTN_GROUPS) * ATT_TQ) == 0
    cosf, sinf = _rope_tables(seq)

    def col(g, t):
        base = col_qkv // dh + (g * 3 + t) * hpg
        return lambda b, j: (b, 0, base + j)

    in_specs = [pl.BlockSpec((None, seq, dh), col(g, t)) for g in range(n_g) for t in range(3)]
    in_specs += [pl.BlockSpec((seq, dh), lambda b, j: (0, 0)) for _ in range(2)]
    pad_rows = max(dil * (seq // dil + ATT_TK) for _, dil in ATTN_GROUPS)
    return pl.pallas_call(
        functools.partial(_attn_kernel, seq=seq),
        grid=(bsz, hpg),
        in_specs=in_specs,
        out_specs=pl.BlockSpec((None, seq, dh), lambda b, j: (b, 0, j)),
        out_shape=jax.ShapeDtypeStruct((bsz, seq, hpg * dh), BF16),
        scratch_shapes=[
            pltpu.VMEM((seq, dh), F32), pltpu.VMEM((seq, dh), F32), pltpu.VMEM((seq, dh), F32),
            pltpu.VMEM((seq, dh), BF16),
            pltpu.VMEM((pad_rows, dh), BF16), pltpu.VMEM((pad_rows, dh), BF16),
            pltpu.VMEM((n_g, seq, dh), F32), pltpu.VMEM((n_g, seq, dh), F32),
        ],
        compiler_params=_params(("parallel", "parallel")),
        name="attn",
    )(*([rest] * (3 * n_g)), cosf, sinf)


def _mix_out_kernel(oa_ref, ob_ref, ga_ref, gb_ref, h_ref, wa_ref, wb_ref, wo_ref, g_ref, b_ref,
                    out_ref, *, alpha):
    ya = _dot(oa_ref[...], wa_ref[...])
    yb = _dot(ob_ref[...], wb_ref[...])
    z = _sigmoid(ga_ref[...].astype(F32)) * ya + _sigmoid(gb_ref[...].astype(F32)) * yb
    mix = _dot(z.astype(BF16), wo_ref[...])
    out_ref[...] = _layer_norm_rows(alpha * h_ref[...] + mix, g_ref[...], b_ref[...])


def mix_out(oa, ob, rest2d, h, wa_bf, wb_bf, wo_bf, g, b, *, alpha, col_gate, tm=256):
    m, d = h.shape
    wa_w = oa.shape[1]
    wb_w = ob.shape[1]
    assert m % tm == 0 and col_gate % d == 0
    const = lambda i: (0, 0)
    single = pl.Buffered(1)
    return pl.pallas_call(
        functools.partial(_mix_out_kernel, alpha=alpha),
        grid=(m // tm,),
        in_specs=[
            pl.BlockSpec((tm, wa_w), lambda i: (i, 0)),
            pl.BlockSpec((tm, wb_w), lambda i: (i, 0)),
            pl.BlockSpec((tm, d), lambda i: (i, col_gate // d)),
            pl.BlockSpec((tm, d), lambda i: (i, col_gate // d + 1)),
            pl.BlockSpec((tm, d), lambda i: (i, 0)),
            pl.BlockSpec((wa_w, d), const, pipeline_mode=single),
            pl.BlockSpec((wb_w, d), const, pipeline_mode=single),
            pl.BlockSpec((d, d), const, pipeline_mode=single),
            pl.BlockSpec((1, d), const),
            pl.BlockSpec((1, d), const),
        ],
        out_specs=pl.BlockSpec((tm, d), lambda i: (i, 0)),
        out_shape=jax.ShapeDtypeStruct((m, d), F32),
        compiler_params=_params(("parallel",)),
        name="mix_out",
    )(oa, ob, rest2d, rest2d, h, wa_bf, wb_bf, wo_bf, g.reshape(1, d), b.reshape(1, d))


def kernel(x, ffn1_w_in, ffn1_w_out, ln1_g, ln1_b, mix_w_in, hgrn_lb_fwd, hgrn_lb_bwd, hgrn_norm_g,
           w_branch_a, w_branch_b, mix_w_out, ln2_g, ln2_b, ffn2_w_in, ffn2_w_out, ln3_g, ln3_b):
    bsz, seq, d = x.shape
    depth = ffn1_w_in.shape[0]
    alpha = (2.0 * depth) ** 0.25
    hw = w_branch_a.shape[1]
    n_heads = hw // HGRN_HEAD_DIM
    qkv_w = len(ATTN_GROUPS) * 3 * ATTN_HEADS_PER_GROUP * HEAD_DIM
    m = bsz * seq

    h = x.reshape(m, d)
    for layer in range(depth):
        h, h_bf = ffn_ln(h, ffn1_w_in[layer].astype(BF16), ffn1_w_out[layer].astype(BF16),
                         ln1_g[layer], ln1_b[layer], alpha=alpha, emit_bf16=True)

        w_in = mix_w_in[layer].astype(BF16)
        w_hf = w_in[:, hw:3 * hw]
        w_rest = jnp.concatenate([w_in[:, :hw], w_in[:, 3 * hw:5 * hw], w_in[:, 5 * hw + qkv_w:],
                                  w_in[:, 5 * hw:5 * hw + qkv_w]], axis=1)
        hf = matmul(h_bf, w_hf, F32).reshape(bsz, seq, 2 * hw)
        rest2d = matmul(h_bf, w_rest, BF16, tm=1024, tn=w_rest.shape[1] // 4)
        rest = rest2d.reshape(bsz, seq, -1)
        col_hq, col_hi, col_hog, col_gate = 0, hw, 2 * hw, 3 * hw
        col_qkv = 3 * hw + 2 * d

        o_a = hgrn(rest, hf, hgrn_lb_fwd, hgrn_lb_bwd, hgrn_norm_g[layer], layer=layer,
                   n_heads=n_heads, col_hq=col_hq, col_hi=col_hi, col_hog=col_hog)
        o_b = attn(rest, col_qkv=col_qkv)

        h = mix_out(o_a.reshape(m, hw), o_b.reshape(m, -1), rest2d, h,
                    w_branch_a[layer].astype(BF16), w_branch_b[layer].astype(BF16),
                    mix_w_out[layer].astype(BF16), ln2_g[layer], ln2_b[layer],
                    alpha=alpha, col_gate=col_gate)

        (h,) = ffn_ln(h, ffn2_w_in[layer].astype(BF16), ffn2_w_out[layer].astype(BF16),
                      ln3_g[layer], ln3_b[layer], alpha=alpha, emit_bf16=False)
    return h.reshape(bsz, seq, d)
```

```python
import functools

import jax
import jax.numpy as jnp
from jax import lax
from jax.experimental import pallas as pl
from jax.experimental.pallas import tpu as pltpu

F32 = jnp.float32
BF16 = jnp.bfloat16

HGRN_HEAD_DIM = 128
HGRN_CHUNK = 32
HGRN_SLAB = 4 * HGRN_CHUNK
HGRN_UNROLL = 16
ATTN_GROUPS = ((128, 1), (512, 4), (2048, 16))
ATTN_HEADS_PER_GROUP = 4
HEAD_DIM = 128
ROPE_THETA = 500000.0
ROPE_DIM = HEAD_DIM // 4
LN_EPS = 1e-5
NEG_INF = -1e30

V7X_VMEM_LIMIT_BYTES = 60000 * 1024


def _params(semantics):
    return pltpu.CompilerParams(dimension_semantics=semantics,
                                vmem_limit_bytes=V7X_VMEM_LIMIT_BYTES)


def _dot(a, b):
    return jnp.dot(a, b, preferred_element_type=F32)


def _dot_nt(a, b):
    return lax.dot_general(a, b, (((1,), (1,)), ((), ())), preferred_element_type=F32)


def _dot_tn(a, b):
    return lax.dot_general(a, b, (((0,), (0,)), ((), ())), preferred_element_type=F32)


def _sigmoid(x):
    return 1.0 / (1.0 + jnp.exp(-x))


def _layer_norm_rows(y, g, b):
    mu = jnp.mean(y, axis=-1, keepdims=True)
    d = y - mu
    var = jnp.mean(d * d, axis=-1, keepdims=True)
    return d * lax.rsqrt(var + LN_EPS) * g + b


def _ffn_ln_kernel(x_ref, wg_ref, wu_ref, wo_ref, g_ref, b_ref, *rest, alpha, n_f):
    (out_ref, *maybe_outbf), (acc_ref, xbf_ref) = rest[:-2], rest[-2:]
    f = pl.program_id(1)

    @pl.when(f == 0)
    def _():
        acc_ref[...] = jnp.zeros_like(acc_ref)
        xbf_ref[...] = x_ref[...].astype(BF16)

    xb = xbf_ref[...]
    gate = _dot(xb, wg_ref[...])
    up = _dot(xb, wu_ref[...])
    hid = (gate * _sigmoid(gate)) * up
    acc_ref[...] += _dot(hid.astype(BF16), wo_ref[...])

    @pl.when(f == n_f - 1)
    def _():
        y = alpha * x_ref[...] + 0.5 * acc_ref[...]
        h = _layer_norm_rows(y, g_ref[...], b_ref[...])
        out_ref[...] = h
        for outbf_ref in maybe_outbf:
            outbf_ref[...] = h.astype(BF16)


def ffn_ln(x, w_in_bf, w_out_bf, g, b, *, alpha, emit_bf16, tm=512, tf=512):
    m, d = x.shape
    ff = w_out_bf.shape[0]
    assert m % tm == 0 and ff % tf == 0
    n_f = ff // tf
    kern = functools.partial(_ffn_ln_kernel, alpha=alpha, n_f=n_f)
    out_dtypes = [F32, BF16] if emit_bf16 else [F32]
    return pl.pallas_call(
        kern,
        grid=(m // tm, n_f),
        in_specs=[
            pl.BlockSpec((tm, d), lambda i, f: (i, 0)),
            pl.BlockSpec((d, tf), lambda i, f: (0, f)),
            pl.BlockSpec((d, tf), lambda i, f: (0, f + n_f)),
            pl.BlockSpec((tf, d), lambda i, f: (f, 0)),
            pl.BlockSpec((1, d), lambda i, f: (0, 0)),
            pl.BlockSpec((1, d), lambda i, f: (0, 0)),
        ],
        out_specs=[pl.BlockSpec((tm, d), lambda i, f: (i, 0)) for _ in out_dtypes],
        out_shape=[jax.ShapeDtypeStruct((m, d), dt) for dt in out_dtypes],
        scratch_shapes=[pltpu.VMEM((tm, d), F32), pltpu.VMEM((tm, d), BF16)],
        compiler_params=_params(("parallel", "arbitrary")),
        name="ffn_ln",
    )(x, w_in_bf, w_in_bf, w_out_bf, g.reshape(1, d), b.reshape(1, d))


def _matmul_kernel(x_ref, w_ref, o_ref):
    o_ref[...] = _dot(x_ref[...], w_ref[...]).astype(o_ref.dtype)


def matmul(x_bf, w_bf, out_dtype, *, tm=2048, tn=512):
    m, k = x_bf.shape
    n = w_bf.shape[1]
    assert m % tm == 0 and n % tn == 0
    return pl.pallas_call(
        _matmul_kernel,
        grid=(m // tm, n // tn),
        in_specs=[pl.BlockSpec((tm, k), lambda i, j: (i, 0)),
                  pl.BlockSpec((k, tn), lambda i, j: (0, j))],
        out_specs=pl.BlockSpec((tm, tn), lambda i, j: (i, j)),
        out_shape=jax.ShapeDtypeStruct((m, n), out_dtype),
        compiler_params=_params(("parallel", "arbitrary")),
        name="in_proj",
    )(x_bf, w_bf)


def _hgrn_kernel(hq_ref, hff_ref, hfb_ref, hip_ref, lbf_ref, lbb_ref, hio_ref, hog_ref, ng_ref, out_ref,
                 *scratch, seq, layer):
    c = HGRN_CHUNK
    slab = HGRN_SLAB
    n_slab = seq // slab
    dh = HGRN_HEAD_DIM
    assert slab == 4 * c
    t = pl.program_id(0)
    n_set = (len(scratch) - 2) // 2
    sets = (scratch[:n_set], scratch[n_set:2 * n_set])
    dec_ref, kv_ref = scratch[2 * n_set:]

    @pl.when(t == 0)
    def _():
        for ref in sets[1]:
            ref[...] = jnp.zeros(ref.shape, ref.dtype)

    def lower_bound(lb_ref):
        v = lb_ref[...].astype(F32)
        e = jnp.exp(v - jnp.max(v, axis=0, keepdims=True))
        sm = e / jnp.sum(e, axis=0, keepdims=True)
        return jnp.sum(sm[: layer + 1], axis=0, keepdims=True)

    lbs = (lower_bound(lbf_ref), lower_bound(lbb_ref))
    hf_refs = (hff_ref, hfb_ref)

    ri = lax.broadcasted_iota(jnp.int32, (slab, slab), 0)
    ci = lax.broadcasted_iota(jnp.int32, (slab, slab), 1)
    same_chunk = (ri // c) == (ci // c)
    same_pair = (ri // (2 * c)) == (ci // (2 * c))
    lower = ci <= ri
    upper = ci >= ri
    row_in_chunk = lax.broadcasted_iota(jnp.int32, (slab, dh), 0) % c

    def prep_slab(s, wset):
        qd_ref, qmix_ref, q64_ref, q128_ref, kdt_ref, kmixt_ref, k64t_ref, _ = wset
        r0 = pl.multiple_of(s * slab, slab)
        rows = pl.ds(r0, slab)
        q = hq_ref[rows, :].astype(F32)
        q = q * _sigmoid(q)
        qd, ke, e = [], [], []
        for d in range(2):
            f = lbs[d] + (1.0 - lbs[d]) * _sigmoid(hf_refs[d][rows, :].astype(F32))
            ecum = f
            for sh in (1, 2, 4, 8, 16):
                if d == 0:
                    ecum = ecum * jnp.where(row_in_chunk >= sh, pltpu.roll(ecum, sh, 0), 1.0)
                else:
                    ecum = ecum * jnp.where(row_in_chunk < c - sh, pltpu.roll(ecum, slab - sh, 0), 1.0)
            tot_row = c - 1 if d == 0 else 0
            e.append([ecum[j * c + tot_row: j * c + tot_row + 1, :] for j in range(4)])
            qd_d = q * ecum
            kd_d = (1.0 - f) / ecum
            qd_ref[d, rows, :] = qd_d.astype(BF16)
            kdt_ref[d, s] = kd_d.T.astype(BF16)
            qd.append([qd_d[j * c:(j + 1) * c] for j in range(4)])
            ke.append([kd_d[j * c:(j + 1) * c] * e[d][j] for j in range(4)])

        cat = lambda pieces: jnp.concatenate(pieces, axis=0)
        qmix_ref[rows, :] = cat([qd[1][0], qd[0][1], qd[1][2], qd[0][3]]).astype(BF16)
        kmixt_ref[s] = cat([ke[0][0], ke[1][1], ke[0][2], ke[1][3]]).T.astype(BF16)
        q64_ref[rows, :] = cat([qd[1][0] * e[1][1], qd[1][1], qd[0][2], qd[0][3] * e[0][2]]).astype(BF16)
        k64t_ref[s] = cat([ke[0][0] * e[0][1], ke[0][1], ke[1][2], ke[1][3] * e[1][2]]).T.astype(BF16)
        k128 = []
        for d in range(2):
            e0, e1, e2, e3 = e[d]
            from_start = (None, e0, e0 * e1, e0 * e1 * e2)
            to_end = (e1 * e2 * e3, e2 * e3, e3, None)
            qmul, kmul = (from_start, to_end) if d == 0 else (to_end, from_start)
            mul = lambda x, m: x if m is None else x * m
            q128_ref[rows, d * dh:(d + 1) * dh] = cat([mul(qd[d][j], qmul[j]) for j in range(4)]).astype(BF16)
            k128.append(cat([mul(ke[d][j], kmul[j]) for j in range(4)]).astype(BF16))
            dec_ref[d, pl.ds(pl.multiple_of(s * 8, 8), 8), :] = jnp.broadcast_to(e0 * e1 * e2 * e3, (8, dh))
        kv_ref[s] = _dot_tn(hip_ref[rows, :], jnp.concatenate(k128, axis=1))

    def state_step(i, st, d, snap_ref):
        s = i if d == 0 else n_slab - 1 - i
        snap_ref[s, :, d * dh:(d + 1) * dh] = st
        return st * dec_ref[d, pl.ds(s * 8, 1), :] + kv_ref[s, :, d * dh:(d + 1) * dh]

    def out_slab(s, rset):
        qd_ref, qmix_ref, q64_ref, q128_ref, kdt_ref, kmixt_ref, k64t_ref, snap_ref = rset
        rows = pl.ds(pl.multiple_of(s * slab, slab), slab)
        m_df = _dot(qd_ref[0, rows, :], kdt_ref[0, s])
        m_db = _dot(qd_ref[1, rows, :], kdt_ref[1, s])
        m_32 = _dot(qmix_ref[rows, :], kmixt_ref[s])
        m_64 = _dot(q64_ref[rows, :], k64t_ref[s])
        a = jnp.where(same_chunk,
                      jnp.where(lower, m_df, 0.0) + jnp.where(upper, m_db, 0.0),
                      jnp.where(same_pair, m_32, m_64))
        states = jnp.concatenate([snap_ref[s, :, :dh].T, snap_ref[s, :, dh:].T], axis=0)
        o = _dot(a.astype(BF16), hio_ref[rows, :]) + _dot(q128_ref[rows, :], states.astype(BF16))
        o = o * lax.rsqrt(jnp.mean(o * o, axis=-1, keepdims=True) + LN_EPS)
        og = hog_ref[rows, :].astype(F32)
        out_ref[rows, :] = (o * ng_ref[...] * (og * _sigmoid(og))).astype(out_ref.dtype)

    def run(wset, rset):
        def both(s, carry):
            prep_slab(s, wset)
            out_slab(s, rset)
            return carry

        lax.fori_loop(0, n_slab, both, 0, unroll=HGRN_UNROLL)
        for d in range(2):
            lax.fori_loop(0, n_slab, functools.partial(state_step, d=d, snap_ref=wset[-1]),
                          jnp.zeros((dh, dh), F32))

    for parity in range(2):
        pl.when(lax.rem(t, 2) == parity)(functools.partial(run, sets[parity], sets[1 - parity]))


def hgrn(rest, hf, lb_fwd, lb_bwd, norm_g, *, layer, n_heads, col_hq, col_hi, col_hog):
    bsz, seq, _ = rest.shape
    dh = HGRN_HEAD_DIM
    width = n_heads * dh
    nl = lb_fwd.shape[0]
    slab = HGRN_SLAB
    assert seq % slab == 0
    n_slab = seq // slab
    n_items = bsz * n_heads
    kern = functools.partial(_hgrn_kernel, seq=seq, layer=layer)

    def prep_item(t):
        return jnp.minimum(t, n_items - 1)

    def out_item(t):
        return jnp.maximum(t - 1, 0)

    def seq_cols(item_of, base):
        return lambda t: (item_of(t) // n_heads, 0, base // dh + item_of(t) % n_heads)

    def head_cols(item_of):
        return lambda t: (0, item_of(t) % n_heads)

    return pl.pallas_call(
        kern,
        grid=(n_items + 1,),
        in_specs=[
            pl.BlockSpec((None, seq, dh), seq_cols(prep_item, col_hq)),
            pl.BlockSpec((None, seq, dh), seq_cols(prep_item, 0)),
            pl.BlockSpec((None, seq, dh), seq_cols(prep_item, width)),
            pl.BlockSpec((None, seq, dh), seq_cols(prep_item, col_hi)),
            pl.BlockSpec((nl, dh), head_cols(prep_item)),
            pl.BlockSpec((nl, dh), head_cols(prep_item)),
            pl.BlockSpec((None, seq, dh), seq_cols(out_item, col_hi)),
            pl.BlockSpec((None, seq, dh), seq_cols(out_item, col_hog)),
            pl.BlockSpec((1, dh), head_cols(out_item)),
        ],
        out_specs=pl.BlockSpec((None, seq, dh), seq_cols(out_item, 0)),
        out_shape=jax.ShapeDtypeStruct((bsz, seq, width), BF16),
        scratch_shapes=2 * [
            pltpu.VMEM((2, seq, dh), BF16),
            pltpu.VMEM((seq, dh), BF16),
            pltpu.VMEM((seq, dh), BF16),
            pltpu.VMEM((seq, 2 * dh), BF16),
            pltpu.VMEM((2, n_slab, dh, slab), BF16),
            pltpu.VMEM((n_slab, dh, slab), BF16),
            pltpu.VMEM((n_slab, dh, slab), BF16),
            pltpu.VMEM((n_slab, dh, 2 * dh), F32),
        ] + [
            pltpu.VMEM((2, n_slab * 8, dh), F32),
            pltpu.VMEM((n_slab, dh, 2 * dh), F32),
        ],
        compiler_params=_params(("arbitrary",)),
        name="hgrn",
    )(rest, hf, hf, rest, lb_fwd, lb_bwd, rest, rest, norm_g.reshape(1, width))


ATT_TQ = 128
ATT_HALF = 64
ATT_TK = ATT_TQ + 2 * ATT_HALF


def _attn_kernel(*refs, seq):
    n_g = len(ATTN_GROUPS)
    qkv_refs = refs[: 3 * n_g]
    cos_ref, sin_ref, out_ref = refs[3 * n_g: 3 * n_g + 3]
    qr_ref, kr_ref, vf_ref, qcm_ref, kcm_ref, vcm_ref, og_ref, lse_ref = refs[3 * n_g + 3:]
    dh = HEAD_DIM
    scale = HEAD_DIM ** -0.5

    lane = lax.broadcasted_iota(jnp.int32, (ATT_TK, dh), 1)

    def rope(x_ref, rows):
        t = x_ref[rows, :].astype(F32)
        swapped = jnp.where(lane < ROPE_DIM // 2,
                            pltpu.roll(t, dh - ROPE_DIM // 2, 1),
                            pltpu.roll(t, ROPE_DIM // 2, 1))
        return t * cos_ref[rows, :] + swapped * sin_ref[rows, :]

    qi_rel = lax.broadcasted_iota(jnp.int32, (ATT_TQ, ATT_TK), 0)
    kj_rel = lax.broadcasted_iota(jnp.int32, (ATT_TQ, ATT_TK), 1) - ATT_HALF
    band = jnp.abs(qi_rel - kj_rel) <= ATT_HALF

    for g, (window, dil) in enumerate(ATTN_GROUPS):
        assert window // (2 * dil) == ATT_HALF
        seg = seq // dil
        n_blk = seg // ATT_TQ
        assert n_blk & (n_blk - 1) == 0
        pitch = seg + ATT_TK
        q_ref, k_ref, v_ref = qkv_refs[3 * g: 3 * g + 3]

        def stage(s, carry, dil=dil, q_ref=q_ref, k_ref=k_ref, v_ref=v_ref):
            rows = pl.ds(pl.multiple_of(s * ATT_TK, ATT_TK), ATT_TK)
            q = rope(q_ref, rows)
            k = rope(k_ref, rows)
            if dil == 1:
                pad_rows = pl.ds(pl.multiple_of(s * ATT_TK + ATT_HALF, ATT_HALF), ATT_TK)
                qcm_ref[rows, :] = q.astype(BF16)
                kcm_ref[pad_rows, :] = k.astype(BF16)
                vcm_ref[pad_rows, :] = v_ref[rows, :]
            else:
                qr_ref[rows, :] = q
                kr_ref[rows, :] = k
                vf_ref[rows, :] = v_ref[rows, :].astype(F32)
            return carry

        lax.fori_loop(0, seq // ATT_TK, stage, 0, unroll=4)
        for cls in range(dil):
            base = cls * pitch
            for pad_ref, src_ref in ((kcm_ref, kr_ref), (vcm_ref, vf_ref)):
                pad_ref[pl.ds(base, ATT_HALF), :] = jnp.zeros((ATT_HALF, dh), BF16)
                pad_ref[pl.ds(base + ATT_HALF + seg, ATT_TK - ATT_HALF), :] = (
                    jnp.zeros((ATT_TK - ATT_HALF, dh), BF16))
                if dil > 1:
                    pad_ref[pl.ds(base + ATT_HALF, seg), :] = (
                        src_ref[pl.ds(cls, seg, stride=dil), :].astype(BF16))
            if dil > 1:
                qcm_ref[pl.ds(cls * seg, seg), :] = qr_ref[pl.ds(cls, seg, stride=dil), :].astype(BF16)

        def block(u, carry, g=g, dil=dil, seg=seg, n_blk=n_blk, pitch=pitch):
            cls = lax.shift_right_logical(u, n_blk.bit_length() - 1)
            q0 = (u & (n_blk - 1)) * ATT_TQ
            if dil == 1:
                tok_rows = pl.ds(pl.multiple_of(q0, ATT_TQ), ATT_TQ)
            else:
                tok_rows = pl.ds(cls + dil * q0, ATT_TQ, stride=dil)
            qb = qcm_ref[pl.ds(pl.multiple_of(u * ATT_TQ, ATT_TQ), ATT_TQ), :]
            win = pl.ds(pl.multiple_of(cls * pitch + q0, ATT_TQ), ATT_TK)
            s = _dot_nt(qb, kcm_ref[win, :]) * scale
            kj = kj_rel + q0
            valid = band & (kj >= 0) & (kj < seg)
            s = jnp.where(valid, s, NEG_INF)
            m = jnp.max(s, axis=-1, keepdims=True)
            p = jnp.exp(s - m)
            denom = jnp.sum(p, axis=-1, keepdims=True)
            o = _dot(p.astype(BF16), vcm_ref[win, :]) / denom
            lse = m + jnp.log(denom)
            og_ref[g, tok_rows, :] = o
            lse_ref[g, tok_rows, :] = jnp.broadcast_to(lse, (ATT_TQ, dh))
            return carry

        lax.fori_loop(0, dil * n_blk, block, 0, unroll=16)

    def combine(s, carry):
        rows = pl.ds(pl.multiple_of(s * ATT_TK, ATT_TK), ATT_TK)
        lses = [lse_ref[g, rows, :] for g in range(n_g)]
        mx = functools.reduce(jnp.maximum, lses)
        ws = [jnp.exp(l - mx) for l in lses]
        wsum = functools.reduce(lambda a, b: a + b, ws)
        acc = ws[0] * og_ref[0, rows, :]
        for g in range(1, n_g):
            acc = acc + ws[g] * og_ref[g, rows, :]
        out_ref[rows, :] = (acc / wsum).astype(out_ref.dtype)
        return carry

    lax.fori_loop(0, seq // ATT_TK, combine, 0, unroll=2)


def _rope_tables(seq):
    inv_freq = ROPE_THETA ** (-jnp.arange(0, ROPE_DIM, 2, dtype=F32) / ROPE_DIM)
    ang = jnp.arange(seq).astype(F32)[:, None] * inv_freq
    cos, sin = jnp.cos(ang), jnp.sin(ang)
    pad1 = jnp.ones((seq, HEAD_DIM - ROPE_DIM), F32)
    pad0 = jnp.zeros((seq, HEAD_DIM - ROPE_DIM), F32)
    return (jnp.concatenate([cos, cos, pad1], axis=1),
            jnp.concatenate([-sin, sin, pad0], axis=1))


def attn(rest, *, col_qkv):
    bsz, seq, _ = rest.shape
    dh = HEAD_DIM
    hpg = ATTN_HEADS_PER_GROUP
    n_g = len(ATTN_GROUPS)
    assert seq % (max(d for _, d in ATTN_GROUPS) * ATT_TQ) == 0
    cosf, sinf = _rope_tables(seq)

    def col(g, t):
        base = col_qkv // dh + (g * 3 + t) * hpg
        return lambda b, j: (b, 0, base + j)

    in_specs = [pl.BlockSpec((None, seq, dh), col(g, t)) for g in range(n_g) for t in range(3)]
    in_specs += [pl.BlockSpec((seq, dh), lambda b, j: (0, 0)) for _ in range(2)]
    pad_rows = max(dil * (seq // dil + ATT_TK) for _, dil in ATTN_GROUPS)
    return pl.pallas_call(
        functools.partial(_attn_kernel, seq=seq),
        grid=(bsz, hpg),
        in_specs=in_specs,
        out_specs=pl.BlockSpec((None, seq, dh), lambda b, j: (b, 0, j)),
        out_shape=jax.ShapeDtypeStruct((bsz, seq, hpg * dh), BF16),
        scratch_shapes=[
            pltpu.VMEM((seq, dh), F32), pltpu.VMEM((seq, dh), F32), pltpu.VMEM((seq, dh), F32),
            pltpu.VMEM((seq, dh), BF16),
            pltpu.VMEM((pad_rows, dh), BF16), pltpu.VMEM((pad_rows, dh), BF16),
            pltpu.VMEM((n_g, seq, dh), F32), pltpu.VMEM((n_g, seq, dh), F32),
        ],
        compiler_params=_params(("parallel", "parallel")),
        name="attn",
    )(*([rest] * (3 * n_g)), cosf, sinf)


def _mix_out_kernel(oa_ref, ob_ref, ga_ref, gb_ref, h_ref, wa_ref, wb_ref, wo_ref, g_ref, b_ref,
                    out_ref, *, alpha):
    ya = _dot(oa_ref[...], wa_ref[...])
    yb = _dot(ob_ref[...], wb_ref[...])
    z = _sigmoid(ga_ref[...].astype(F32)) * ya + _sigmoid(gb_ref[...].astype(F32)) * yb
    mix = _dot(z.astype(BF16), wo_ref[...])
    out_ref[...] = _layer_norm_rows(alpha * h_ref[...] + mix, g_ref[...], b_ref[...])


def mix_out(oa, ob, rest2d, h, wa_bf, wb_bf, wo_bf, g, b, *, alpha, col_gate, tm=256):
    m, d = h.shape
    wa_w = oa.shape[1]
    wb_w = ob.shape[1]
    assert m % tm == 0 and col_gate % d == 0
    const = lambda i: (0, 0)
    single = pl.Buffered(1)
    return pl.pallas_call(
        functools.partial(_mix_out_kernel, alpha=alpha),
        grid=(m // tm,),
        in_specs=[
            pl.BlockSpec((tm, wa_w), lambda i: (i, 0)),
            pl.BlockSpec((tm, wb_w), lambda i: (i, 0)),
            pl.BlockSpec((tm, d), lambda i: (i, col_gate // d)),
            pl.BlockSpec((tm, d), lambda i: (i, col_gate // d + 1)),
            pl.BlockSpec((tm, d), lambda i: (i, 0)),
            pl.BlockSpec((wa_w, d), const, pipeline_mode=single),
            pl.BlockSpec((wb_w, d), const, pipeline_mode=single),
            pl.BlockSpec((d, d), const, pipeline_mode=single),
            pl.BlockSpec((1, d), const),
            pl.BlockSpec((1, d), const),
        ],
        out_specs=pl.BlockSpec((tm, d), lambda i: (i, 0)),
        out_shape=jax.ShapeDtypeStruct((m, d), F32),
        compiler_params=_params(("parallel",)),
        name="mix_out",
    )(oa, ob, rest2d, rest2d, h, wa_bf, wb_bf, wo_bf, g.reshape(1, d), b.reshape(1, d))


def kernel(x, ffn1_w_in, ffn1_w_out, ln1_g, ln1_b, mix_w_in, hgrn_lb_fwd, hgrn_lb_bwd, hgrn_norm_g,
           w_branch_a, w_branch_b, mix_w_out, ln2_g, ln2_b, ffn2_w_in, ffn2_w_out, ln3_g, ln3_b):
    bsz, seq, d = x.shape
    depth = ffn1_w_in.shape[0]
    alpha = (2.0 * depth) ** 0.25
    hw = w_branch_a.shape[1]
    n_heads = hw // HGRN_HEAD_DIM
    qkv_w = len(ATTN_GROUPS) * 3 * ATTN_HEADS_PER_GROUP * HEAD_DIM
    m = bsz * seq

    h = x.reshape(m, d)
    for layer in range(depth):
        h, h_bf = ffn_ln(h, ffn1_w_in[layer].astype(BF16), ffn1_w_out[layer].astype(BF16),
                         ln1_g[layer], ln1_b[layer], alpha=alpha, emit_bf16=True)

        w_in = mix_w_in[layer]
        w_hf = w_in[:, hw:3 * hw].astype(BF16)
        w_rest = jnp.concatenate([w_in[:, :hw], w_in[:, 3 * hw:5 * hw], w_in[:, 5 * hw + qkv_w:],
                                  w_in[:, 5 * hw:5 * hw + qkv_w]], axis=1).astype(BF16)
        hf = matmul(h_bf, w_hf, F32).reshape(bsz, seq, 2 * hw)
        rest2d = matmul(h_bf, w_rest, BF16, tm=1024, tn=w_rest.shape[1] // 4)
        rest = rest2d.reshape(bsz, seq, -1)
        col_hq, col_hi, col_hog, col_gate = 0, hw, 2 * hw, 3 * hw
        col_qkv = 3 * hw + 2 * d

        o_a = hgrn(rest, hf, hgrn_lb_fwd, hgrn_lb_bwd, hgrn_norm_g[layer], layer=layer,
                   n_heads=n_heads, col_hq=col_hq, col_hi=col_hi, col_hog=col_hog)
        o_b = attn(rest, col_qkv=col_qkv)

        h = mix_out(o_a.reshape(m, hw), o_b.reshape(m, -1), rest2d, h,
                    w_branch_a[layer].astype(BF16), w_branch_b[layer].astype(BF16),
                    mix_w_out[layer].astype(BF16), ln2_g[layer], ln2_b[layer],
                    alpha=alpha, col_gate=col_gate)

        (h,) = ffn_ln(h, ffn2_w_in[layer].astype(BF16), ffn2_w_out[layer].astype(BF16),
                      ln3_g[layer], ln3_b[layer], alpha=alpha, emit_bf16=False)
    return h.reshape(bsz, seq, d)
```

```python
import functools

import jax
import jax.numpy as jnp
from jax import lax
from jax.experimental import pallas as pl
from jax.experimental.pallas import tpu as pltpu

F32 = jnp.float32
BF16 = jnp.bfloat16

HGRN_HEAD_DIM = 128
HGRN_CHUNK = 32
HGRN_SLAB = 4 * HGRN_CHUNK
HGRN_UNROLL = 16
ATTN_GROUPS = ((128, 1), (512, 4), (2048, 16))
ATTN_HEADS_PER_GROUP = 4
HEAD_DIM = 128
ROPE_THETA = 500000.0
ROPE_DIM = HEAD_DIM // 4
LN_EPS = 1e-5
NEG_INF = -1e30

V7X_VMEM_LIMIT_BYTES = 60000 * 1024


def _params(semantics):
    return pltpu.CompilerParams(dimension_semantics=semantics,
                                vmem_limit_bytes=V7X_VMEM_LIMIT_BYTES)


def _dot(a, b):
    return jnp.dot(a, b, preferred_element_type=F32)


def _dot_nt(a, b):
    return lax.dot_general(a, b, (((1,), (1,)), ((), ())), preferred_element_type=F32)


def _dot_tn(a, b):
    return lax.dot_general(a, b, (((0,), (0,)), ((), ())), preferred_element_type=F32)


def _sigmoid(x):
    return 1.0 / (1.0 + jnp.exp(-x))


def _layer_norm_rows(y, g, b):
    mu = jnp.mean(y, axis=-1, keepdims=True)
    d = y - mu
    var = jnp.mean(d * d, axis=-1, keepdims=True)
    return d * lax.rsqrt(var + LN_EPS) * g + b


def _ffn_ln_kernel(x_ref, wg_ref, wu_ref, wo_ref, g_ref, b_ref, *rest, alpha, n_f):
    (out_ref, *maybe_outbf), (acc_ref, xbf_ref) = rest[:-2], rest[-2:]
    f = pl.program_id(1)

    @pl.when(f == 0)
    def _():
        acc_ref[...] = jnp.zeros_like(acc_ref)
        xbf_ref[...] = x_ref[...].astype(BF16)

    xb = xbf_ref[...]
    gate = _dot(xb, wg_ref[...])
    up = _dot(xb, wu_ref[...])
    hid = (gate * _sigmoid(gate)) * up
    acc_ref[...] += _dot(hid.astype(BF16), wo_ref[...])

    @pl.when(f == n_f - 1)
    def _():
        y = alpha * x_ref[...] + 0.5 * acc_ref[...]
        h = _layer_norm_rows(y, g_ref[...], b_ref[...])
        out_ref[...] = h
        for outbf_ref in maybe_outbf:
            outbf_ref[...] = h.astype(BF16)


def ffn_ln(x, w_in_bf, w_out_bf, g, b, *, alpha, emit_bf16, tm=512, tf=512):
    m, d = x.shape
    ff = w_out_bf.shape[0]
    assert m % tm == 0 and ff % tf == 0
    n_f = ff // tf
    kern = functools.partial(_ffn_ln_kernel, alpha=alpha, n_f=n_f)
    out_dtypes = [F32, BF16] if emit_bf16 else [F32]
    return pl.pallas_call(
        kern,
        grid=(m // tm, n_f),
        in_specs=[
            pl.BlockSpec((tm, d), lambda i, f: (i, 0)),
            pl.BlockSpec((d, tf), lambda i, f: (0, f)),
            pl.BlockSpec((d, tf), lambda i, f: (0, f + n_f)),
            pl.BlockSpec((tf, d), lambda i, f: (f, 0)),
            pl.BlockSpec((1, d), lambda i, f: (0, 0)),
            pl.BlockSpec((1, d), lambda i, f: (0, 0)),
        ],
        out_specs=[pl.BlockSpec((tm, d), lambda i, f: (i, 0)) for _ in out_dtypes],
        out_shape=[jax.ShapeDtypeStruct((m, d), dt) for dt in out_dtypes],
        scratch_shapes=[pltpu.VMEM((tm, d), F32), pltpu.VMEM((tm, d), BF16)],
        compiler_params=_params(("parallel", "arbitrary")),
        name="ffn_ln",
    )(x, w_in_bf, w_in_bf, w_out_bf, g.reshape(1, d), b.reshape(1, d))


def _matmul_kernel(x_ref, w_ref, o_ref):
    o_ref[...] = _dot(x_ref[...], w_ref[...]).astype(o_ref.dtype)


def matmul(x_bf, w_bf, out_dtype, *, tm=2048, tn=512):
    m, k = x_bf.shape
    n = w_bf.shape[1]
    assert m % tm == 0 and n % tn == 0
    return pl.pallas_call(
        _matmul_kernel,
        grid=(m // tm, n // tn),
        in_specs=[pl.BlockSpec((tm, k), lambda i, j: (i, 0)),
                  pl.BlockSpec((k, tn), lambda i, j: (0, j))],
        out_specs=pl.BlockSpec((tm, tn), lambda i, j: (i, j)),
        out_shape=jax.ShapeDtypeStruct((m, n), out_dtype),
        compiler_params=_params(("parallel", "arbitrary")),
        name="in_proj",
    )(x_bf, w_bf)


def _hgrn_kernel(hq_ref, hff_ref, hfb_ref, hip_ref, lbf_ref, lbb_ref, hio_ref, hog_ref, ng_ref, out_ref,
                 *scratch, seq, layer):
    c = HGRN_CHUNK
    slab = HGRN_SLAB
    n_slab = seq // slab
    dh = HGRN_HEAD_DIM
    assert slab == 4 * c
    t = pl.program_id(0)
    n_set = (len(scratch) - 2) // 2
    sets = (scratch[:n_set], scratch[n_set:2 * n_set])
    dec_ref, kv_ref = scratch[2 * n_set:]

    @pl.when(t == 0)
    def _():
        for ref in sets[1]:
            ref[...] = jnp.zeros(ref.shape, ref.dtype)

    def lower_bound(lb_ref):
        v = lb_ref[...].astype(F32)
        e = jnp.exp(v - jnp.max(v, axis=0, keepdims=True))
        sm = e / jnp.sum(e, axis=0, keepdims=True)
        return jnp.sum(sm[: layer + 1], axis=0, keepdims=True)

    lbs = (lower_bound(lbf_ref), lower_bound(lbb_ref))
    hf_refs = (hff_ref, hfb_ref)

    ri = lax.broadcasted_iota(jnp.int32, (slab, slab), 0)
    ci = lax.broadcasted_iota(jnp.int32, (slab, slab), 1)
    same_chunk = (ri // c) == (ci // c)
    same_pair = (ri // (2 * c)) == (ci // (2 * c))
    lower = ci <= ri
    upper = ci >= ri
    row_in_chunk = lax.broadcasted_iota(jnp.int32, (slab, dh), 0) % c

    def prep_slab(s, wset):
        qd_ref, qmix_ref, q64_ref, q128_ref, kdt_ref, kmixt_ref, k64t_ref, _ = wset
        r0 = pl.multiple_of(s * slab, slab)
        rows = pl.ds(r0, slab)
        q = hq_ref[rows, :].astype(F32)
        q = q * _sigmoid(q)
        qd, ke, e = [], [], []
        for d in range(2):
            f = lbs[d] + (1.0 - lbs[d]) * _sigmoid(hf_refs[d][rows, :].astype(F32))
            ecum = f
            for sh in (1, 2, 4, 8, 16):
                if d == 0:
                    ecum = ecum * jnp.where(row_in_chunk >= sh, pltpu.roll(ecum, sh, 0), 1.0)
                else:
                    ecum = ecum * jnp.where(row_in_chunk < c - sh, pltpu.roll(ecum, slab - sh, 0), 1.0)
            tot_row = c - 1 if d == 0 else 0
            e.append([ecum[j * c + tot_row: j * c + tot_row + 1, :] for j in range(4)])
            qd_d = q * ecum
            kd_d = (1.0 - f) / ecum
            qd_ref[d, rows, :] = qd_d.astype(BF16)
            kdt_ref[d, s] = kd_d.T.astype(BF16)
            qd.append([qd_d[j * c:(j + 1) * c] for j in range(4)])
            ke.append([kd_d[j * c:(j + 1) * c] * e[d][j] for j in range(4)])

        cat = lambda pieces: jnp.concatenate(pieces, axis=0)
        qmix_ref[rows, :] = cat([qd[1][0], qd[0][1], qd[1][2], qd[0][3]]).astype(BF16)
        kmixt_ref[s] = cat([ke[0][0], ke[1][1], ke[0][2], ke[1][3]]).T.astype(BF16)
        q64_ref[rows, :] = cat([qd[1][0] * e[1][1], qd[1][1], qd[0][2], qd[0][3] * e[0][2]]).astype(BF16)
        k64t_ref[s] = cat([ke[0][0] * e[0][1], ke[0][1], ke[1][2], ke[1][3] * e[1][2]]).T.astype(BF16)
        k128 = []
        for d in range(2):
            e0, e1, e2, e3 = e[d]
            from_start = (None, e0, e0 * e1, e0 * e1 * e2)
            to_end = (e1 * e2 * e3, e2 * e3, e3, None)
            qmul, kmul = (from_start, to_end) if d == 0 else (to_end, from_start)
            mul = lambda x, m: x if m is None else x * m
            q128_ref[rows, d * dh:(d + 1) * dh] = cat([mul(qd[d][j], qmul[j]) for j in range(4)]).astype(BF16)
            k128.append(cat([mul(ke[d][j], kmul[j]) for j in range(4)]).astype(BF16))
            dec_ref[d, pl.ds(pl.multiple_of(s * 8, 8), 8), :] = jnp.broadcast_to(e0 * e1 * e2 * e3, (8, dh))
        kv_ref[s] = _dot_tn(hip_ref[rows, :], jnp.concatenate(k128, axis=1))

    def state_step(i, st, d, snap_ref):
        s = i if d == 0 else n_slab - 1 - i
        snap_ref[s, :, d * dh:(d + 1) * dh] = st
        return st * dec_ref[d, pl.ds(s * 8, 1), :] + kv_ref[s, :, d * dh:(d + 1) * dh]

    def out_slab(s, rset):
        qd_ref, qmix_ref, q64_ref, q128_ref, kdt_ref, kmixt_ref, k64t_ref, snap_ref = rset
        rows = pl.ds(pl.multiple_of(s * slab, slab), slab)
        m_df = _dot(qd_ref[0, rows, :], kdt_ref[0, s])
        m_db = _dot(qd_ref[1, rows, :], kdt_ref[1, s])
        m_32 = _dot(qmix_ref[rows, :], kmixt_ref[s])
        m_64 = _dot(q64_ref[rows, :], k64t_ref[s])
        a = jnp.where(same_chunk,
                      jnp.where(lower, m_df, 0.0) + jnp.where(upper, m_db, 0.0),
                      jnp.where(same_pair, m_32, m_64))
        states = jnp.concatenate([snap_ref[s, :, :dh].T, snap_ref[s, :, dh:].T], axis=0)
        o = _dot(a.astype(BF16), hio_ref[rows, :]) + _dot(q128_ref[rows, :], states.astype(BF16))
        o = o * lax.rsqrt(jnp.mean(o * o, axis=-1, keepdims=True) + LN_EPS)
        og = hog_ref[rows, :].astype(F32)
        out_ref[rows, :] = (o * ng_ref[...] * (og * _sigmoid(og))).astype(out_ref.dtype)

    def run(wset, rset):
        def both(s, carry):
            prep_slab(s, wset)
            out_slab(s, rset)
            return carry

        lax.fori_loop(0, n_slab, both, 0, unroll=HGRN_UNROLL)
        for d in range(2):
            lax.fori_loop(0, n_slab, functools.partial(state_step, d=d, snap_ref=wset[-1]),
                          jnp.zeros((dh, dh), F32))

    for parity in range(2):
        pl.when(lax.rem(t, 2) == parity)(functools.partial(run, sets[parity], sets[1 - parity]))


def hgrn(rest, hf, lb_fwd, lb_bwd, norm_g, *, layer, n_heads, col_hq, col_hi, col_hog):
    bsz, seq, _ = rest.shape
    dh = HGRN_HEAD_DIM
    width = n_heads * dh
    nl = lb_fwd.shape[0]
    slab = HGRN_SLAB
    assert seq % slab == 0
    n_slab = seq // slab
    n_items = bsz * n_heads
    kern = functools.partial(_hgrn_kernel, seq=seq, layer=layer)

    def prep_item(t):
        return jnp.minimum(t, n_items - 1)

    def out_item(t):
        return jnp.maximum(t - 1, 0)

    def seq_cols(item_of, base):
        return lambda t: (item_of(t) // n_heads, 0, base // dh + item_of(t) % n_heads)

    def head_cols(item_of):
        return lambda t: (0, item_of(t) % n_heads)

    return pl.pallas_call(
        kern,
        grid=(n_items + 1,),
        in_specs=[
            pl.BlockSpec((None, seq, dh), seq_cols(prep_item, col_hq)),
            pl.BlockSpec((None, seq, dh), seq_cols(prep_item, 0)),
            pl.BlockSpec((None, seq, dh), seq_cols(prep_item, width)),
            pl.BlockSpec((None, seq, dh), seq_cols(prep_item, col_hi)),
            pl.BlockSpec((nl, dh), head_cols(prep_item)),
            pl.BlockSpec((nl, dh), head_cols(prep_item)),
            pl.BlockSpec((None, seq, dh), seq_cols(out_item, col_hi)),
            pl.BlockSpec((None, seq, dh), seq_cols(out_item, col_hog)),
            pl.BlockSpec((1, dh), head_cols(out_item)),
        ],
        out_specs=pl.BlockSpec((None, seq, dh), seq_cols(out_item, 0)),
        out_shape=jax.ShapeDtypeStruct((bsz, seq, width), BF16),
        scratch_shapes=2 * [
            pltpu.VMEM((2, seq, dh), BF16),
            pltpu.VMEM((seq, dh), BF16),
            pltpu.VMEM((seq, dh), BF16),
            pltpu.VMEM((seq, 2 * dh), BF16),
            pltpu.VMEM((2, n_slab, dh, slab), BF16),
            pltpu.VMEM((n_slab, dh, slab), BF16),
            pltpu.VMEM((n_slab, dh, slab), BF16),
            pltpu.VMEM((n_slab, dh, 2 * dh), F32),
        ] + [
            pltpu.VMEM((2, n_slab * 8, dh), F32),
            pltpu.VMEM((n_slab, dh, 2 * dh), F32),
        ],
        compiler_params=_params(("arbitrary",)),
        name="hgrn",
    )(rest, hf, hf, rest, lb_fwd, lb_bwd, rest, rest, norm_g.reshape(1, width))


ATT_TQ = 128
ATT_HALF = 64
ATT_TK = ATT_TQ + 2 * ATT_HALF


def _attn_kernel(*refs, seq):
    n_g = len(ATTN_GROUPS)
    qkv_refs = refs[: 3 * n_g]
    cos_ref, sin_ref, out_ref = refs[3 * n_g: 3 * n_g + 3]
    qr_ref, kr_ref, vf_ref, qcm_ref, kcm_ref, vcm_ref, og_ref, lse_ref = refs[3 * n_g + 3:]
    dh = HEAD_DIM
    scale = HEAD_DIM ** -0.5

    lane = lax.broadcasted_iota(jnp.int32, (ATT_TK, dh), 1)

    def rope(x_ref, rows):
        t = x_ref[rows, :].astype(F32)
        swapped = jnp.where(lane < ROPE_DIM // 2,
                            pltpu.roll(t, dh - ROPE_DIM // 2, 1),
                            pltpu.roll(t, ROPE_DIM // 2, 1))
        return t * cos_ref[rows, :] + swapped * sin_ref[rows, :]

    qi_rel = lax.broadcasted_iota(jnp.int32, (ATT_TQ, ATT_TK), 0)
    kj_rel = lax.broadcasted_iota(jnp.int32, (ATT_TQ, ATT_TK), 1) - ATT_HALF
    band = jnp.abs(qi_rel - kj_rel) <= ATT_HALF

    for g, (window, dil) in enumerate(ATTN_GROUPS):
        assert window // (2 * dil) == ATT_HALF
        seg = seq // dil
        n_blk = seg // ATT_TQ
        assert n_blk & (n_blk - 1) == 0
        pitch = seg + ATT_TK
        q_ref, k_ref, v_ref = qkv_refs[3 * g: 3 * g + 3]

        def stage(s, carry, dil=dil, q_ref=q_ref, k_ref=k_ref, v_ref=v_ref):
            rows = pl.ds(pl.multiple_of(s * ATT_TK, ATT_TK), ATT_TK)
            q = rope(q_ref, rows)
            k = rope(k_ref, rows)
            if dil == 1:
                pad_rows = pl.ds(pl.multiple_of(s * ATT_TK + ATT_HALF, ATT_HALF), ATT_TK)
                qcm_ref[rows, :] = q.astype(BF16)
                kcm_ref[pad_rows, :] = k.astype(BF16)
                vcm_ref[pad_rows, :] = v_ref[rows, :]
            else:
                qr_ref[rows, :] = q
                kr_ref[rows, :] = k
                vf_ref[rows, :] = v_ref[rows, :].astype(F32)
            return carry

        lax.fori_loop(0, seq // ATT_TK, stage, 0, unroll=4)
        for cls in range(dil):
            base = cls * pitch
            for pad_ref, src_ref in ((kcm_ref, kr_ref), (vcm_ref, vf_ref)):
                pad_ref[pl.ds(base, ATT_HALF), :] = jnp.zeros((ATT_HALF, dh), BF16)
                pad_ref[pl.ds(base + ATT_HALF + seg, ATT_TK - ATT_HALF), :] = (
                    jnp.zeros((ATT_TK - ATT_HALF, dh), BF16))
                if dil > 1:
                    pad_ref[pl.ds(base + ATT_HALF, seg), :] = (
                        src_ref[pl.ds(cls, seg, stride=dil), :].astype(BF16))
            if dil > 1:
                qcm_ref[pl.ds(cls * seg, seg), :] = qr_ref[pl.ds(cls, seg, stride=dil), :].astype(BF16)

        def block(u, carry, g=g, dil=dil, seg=seg, n_blk=n_blk, pitch=pitch):
            cls = lax.shift_right_logical(u, n_blk.bit_length() - 1)
            q0 = (u & (n_blk - 1)) * ATT_TQ
            if dil == 1:
                tok_rows = pl.ds(pl.multiple_of(q0, ATT_TQ), ATT_TQ)
            else:
                tok_rows = pl.ds(cls + dil * q0, ATT_TQ, stride=dil)
            qb = qcm_ref[pl.ds(pl.multiple_of(u * ATT_TQ, ATT_TQ), ATT_TQ), :]
            win = pl.ds(pl.multiple_of(cls * pitch + q0, ATT_TQ), ATT_TK)
            s = _dot_nt(qb, kcm_ref[win, :]) * scale
            kj = kj_rel + q0
            valid = band & (kj >= 0) & (kj < seg)
            s = jnp.where(valid, s, NEG_INF)
            m = jnp.max(s, axis=-1, keepdims=True)
            p = jnp.exp(s - m)
            denom = jnp.sum(p, axis=-1, keepdims=True)
            o = _dot(p.astype(BF16), vcm_ref[win, :]) / denom
            lse = m + jnp.log(denom)
            og_ref[g, tok_rows, :] = o
            lse_ref[g, tok_rows, :] = jnp.broadcast_to(lse, (ATT_TQ, dh))
            return carry

        lax.fori_loop(0, dil * n_blk, block, 0, unroll=16)

    def combine(s, carry):
        rows = pl.ds(pl.multiple_of(s * ATT_TK, ATT_TK), ATT_TK)
        lses = [lse_ref[g, rows, :] for g in range(n_g)]
        mx = functools.reduce(jnp.maximum, lses)
        ws = [jnp.exp(l - mx) for l in lses]
        wsum = functools.reduce(lambda a, b: a + b, ws)
        acc = ws[0] * og_ref[0, rows, :]
        for g in range(1, n_g):
            acc = acc + ws[g] * og_ref[g, rows, :]
        out_ref[rows, :] = (acc / wsum).astype(out_ref.dtype)
        return carry

    lax.fori_loop(0, seq // ATT_TK, combine, 0, unroll=2)


def _rope_tables(seq):
    inv_freq = ROPE_THETA ** (-jnp.arange(0, ROPE_DIM, 2, dtype=F32) / ROPE_DIM)
    ang = jnp.arange(seq).astype(F32)[:, None] * inv_freq
    cos, sin = jnp.cos(ang), jnp.sin(ang)
    pad1 = jnp.ones((seq, HEAD_DIM - ROPE_DIM), F32)
    pad0 = jnp.zeros((seq, HEAD_DIM - ROPE_DIM), F32)
    return (jnp.concatenate([cos, cos, pad1], axis=1),
            jnp.concatenate([-sin, sin, pad0], axis=1))


def attn(rest, *, col_qkv):
    bsz, seq, _ = rest.shape
    dh = HEAD_DIM
    hpg = ATTN_HEADS_PER_GROUP
    n_g = len(ATTN_GROUPS)
    assert seq % (max(d for _, d in ATTN_GROUPS) * ATT_TQ) == 0
    cosf, sinf = _rope_tables(seq)

    def col(g, t):
        base = col_qkv // dh + (g * 3 + t) * hpg
        return lambda b, j: (b, 0, base + j)

    in_specs = [pl.BlockSpec((None, seq, dh), col(g, t)) for g in range(n_g) for t in range(3)]
    in_specs += [pl.BlockSpec((seq, dh), lambda b, j: (0, 0)) for _ in range(2)]
    pad_rows = max(dil * (seq // dil + ATT_TK) for _, dil in ATTN_GROUPS)
    return pl.pallas_call(
        functools.partial(_attn_kernel, seq=seq),
        grid=(bsz, hpg),
        in_specs=in_specs,
        out_specs=pl.BlockSpec((None, seq, dh), lambda b, j: (b, 0, j)),
        out_shape=jax.ShapeDtypeStruct((bsz, seq, hpg * dh), BF16),
        scratch_shapes=[
            pltpu.VMEM((seq, dh), F32), pltpu.VMEM((seq, dh), F32), pltpu.VMEM((seq, dh), F32),
            pltpu.VMEM((seq, dh), BF16),
            pltpu.VMEM((pad_rows, dh), BF16), pltpu.VMEM((pad_rows, dh), BF16),
            pltpu.VMEM((n_g, seq, dh), F32), pltpu.VMEM((n_g, seq, dh), F32),
        ],
        compiler_params=_params(("parallel", "parallel")),
        name="attn",
    )(*([rest] * (3 * n_g)), cosf, sinf)


def _mix_out_kernel(oa_ref, ob_ref, ga_ref, gb_ref, h_ref, wa_ref, wb_ref, wo_ref, g_ref, b_ref,
                    out_ref, *, alpha):
    ya = _dot(oa_ref[...], wa_ref[...])
    yb = _dot(ob_ref[...], wb_ref[...])
    z = _sigmoid(ga_ref[...].astype(F32)) * ya + _sigmoid(gb_ref[...].astype(F32)) * yb
    mix = _dot(z.astype(BF16), wo_ref[...])
    out_ref[...] = _layer_norm_rows(alpha * h_ref[...] + mix, g_ref[...], b_ref[...])


def mix_out(oa, ob, rest2d, h, wa_bf, wb_bf, wo_bf, g, b, *, alpha, col_gate, tm=256):
    m, d = h.shape
    wa_w = oa.shape[1]
    wb_w = ob.shape[1]
    assert m % tm == 0 and col_gate % d == 0
    const = lambda i: (0, 0)
    single = pl.Buffered(1)
    return pl.pallas_call(
        functools.partial(_mix_out_kernel, alpha=alpha),
        grid=(m // tm,),
        in_specs=[
            pl.BlockSpec((tm, wa_w), lambda i: (i, 0)),
            pl.BlockSpec((tm, wb_w), lambda i: (i, 0)),
            pl.BlockSpec((tm, d), lambda i: (i, col_gate // d)),
            pl.BlockSpec((tm, d), lambda i: (i, col_gate // d + 1)),
            pl.BlockSpec((tm, d), lambda i: (i, 0)),
            pl.BlockSpec((wa_w, d), const, pipeline_mode=single),
            pl.BlockSpec((wb_w, d), const, pipeline_mode=single),
            pl.BlockSpec((d, d), const, pipeline_mode=single),
            pl.BlockSpec((1, d), const),
            pl.BlockSpec((1, d), const),
        ],
        out_specs=pl.BlockSpec((tm, d), lambda i: (i, 0)),
        out_shape=jax.ShapeDtypeStruct((m, d), F32),
        compiler_params=_params(("parallel",)),
        name="mix_out",
    )(oa, ob, rest2d, rest2d, h, wa_bf, wb_bf, wo_bf, g.reshape(1, d), b.reshape(1, d))


def kernel(x, ffn1_w_in, ffn1_w_out, ln1_g, ln1_b, mix_w_in, hgrn_lb_fwd, hgrn_lb_bwd, hgrn_norm_g,
           w_branch_a, w_branch_b, mix_w_out, ln2_g, ln2_b, ffn2_w_in, ffn2_w_out, ln3_g, ln3_b):
    bsz, seq, d = x.shape
    depth = ffn1_w_in.shape[0]
    alpha = (2.0 * depth) ** 0.25
    hw = w_branch_a.shape[1]
    n_heads = hw // HGRN_HEAD_DIM
    qkv_w = len(ATTN_GROUPS) * 3 * ATTN_HEADS_PER_GROUP * HEAD_DIM
    m = bsz * seq

    h = x.reshape(m, d)
    for layer in range(depth):
        h, h_bf = ffn_ln(h, ffn1_w_in[layer].astype(BF16), ffn1_w_out[layer].astype(BF16),
                         ln1_g[layer], ln1_b[layer], alpha=alpha, emit_bf16=True)

        w_in = mix_w_in[layer]
        w_hf = w_in[:, hw:3 * hw].astype(BF16)
        w_rest = jnp.concatenate([w_in[:, :hw], w_in[:, 3 * hw:5 * hw], w_in[:, 5 * hw + qkv_w:],
                                  w_in[:, 5 * hw:5 * hw + qkv_w]], axis=1).astype(BF16)
        hf = matmul(h_bf, w_hf, BF16).reshape(bsz, seq, 2 * hw)
        rest2d = matmul(h_bf, w_rest, BF16, tm=1024, tn=w_rest.shape[1] // 4)
        rest = rest2d.reshape(bsz, seq, -1)
        col_hq, col_hi, col_hog, col_gate = 0, hw, 2 * hw, 3 * hw
        col_qkv = 3 * hw + 2 * d

        o_a = hgrn(rest, hf, hgrn_lb_fwd, hgrn_lb_bwd, hgrn_norm_g[layer], layer=layer,
                   n_heads=n_heads, col_hq=col_hq, col_hi=col_hi, col_hog=col_hog)
        o_b = attn(rest, col_qkv=col_qkv)

        h = mix_out(o_a.reshape(m, hw), o_b.reshape(m, -1), rest2d, h,
                    w_branch_a[layer].astype(BF16), w_branch_b[layer].astype(BF16),
                    mix_w_out[layer].astype(BF16), ln2_g[layer], ln2_b[layer],
                    alpha=alpha, col_gate=col_gate)

        (h,) = ffn_ln(h, ffn2_w_in[layer].astype(BF16), ffn2_w_out[layer].astype(BF16),
                      ln3_g[layer], ln3_b[layer], alpha=alpha, emit_bf16=False)
    return h.reshape(bsz, seq, d)
```

```python
import functools

import jax
import jax.numpy as jnp
from jax import lax
from jax.experimental import pallas as pl
from jax.experimental.pallas import tpu as pltpu

F32 = jnp.float32
BF16 = jnp.bfloat16

HGRN_HEAD_DIM = 128
HGRN_CHUNK = 32
HGRN_SLAB = 4 * HGRN_CHUNK
HGRN_UNROLL = 16
ATTN_GROUPS = ((128, 1), (512, 4), (2048, 16))
ATTN_HEADS_PER_GROUP = 4
HEAD_DIM = 128
ROPE_THETA = 500000.0
ROPE_DIM = HEAD_DIM // 4
LN_EPS = 1e-5
NEG_INF = -1e30

V7X_VMEM_LIMIT_BYTES = 60000 * 1024


def _params(semantics):
    return pltpu.CompilerParams(dimension_semantics=semantics,
                                vmem_limit_bytes=V7X_VMEM_LIMIT_BYTES)


def _dot(a, b):
    return jnp.dot(a, b, preferred_element_type=F32)


def _dot_nt(a, b):
    return lax.dot_general(a, b, (((1,), (1,)), ((), ())), preferred_element_type=F32)


def _dot_tn(a, b):
    return lax.dot_general(a, b, (((0,), (0,)), ((), ())), preferred_element_type=F32)


def _sigmoid(x):
    return 1.0 / (1.0 + jnp.exp(-x))


def _layer_norm_rows(y, g, b):
    mu = jnp.mean(y, axis=-1, keepdims=True)
    d = y - mu
    var = jnp.mean(d * d, axis=-1, keepdims=True)
    return d * lax.rsqrt(var + LN_EPS) * g + b


FFN_WEIGHT_SLOTS = 3


def _ffn_ln_kernel(x_ref, win_hbm, wout_hbm, g_ref, b_ref, *rest, alpha, n_f, n_tiles, tf):
    (out_ref, *maybe_outbf), (acc_ref, xbf_ref, wg_buf, wu_buf, wo_buf, sems) = rest[:-6], rest[-6:]
    i = pl.program_id(0)
    ff = n_f * tf

    def copies(f):
        slot = f % FFN_WEIGHT_SLOTS
        return (
            pltpu.make_async_copy(win_hbm.at[:, pl.ds(f * tf, tf)], wg_buf.at[slot], sems.at[0, slot]),
            pltpu.make_async_copy(win_hbm.at[:, pl.ds(ff + f * tf, tf)], wu_buf.at[slot], sems.at[1, slot]),
            pltpu.make_async_copy(wout_hbm.at[pl.ds(f * tf, tf), :], wo_buf.at[slot], sems.at[2, slot]),
        )

    def start(f):
        for cp in copies(f):
            cp.start()

    pl.when(i == 0)(functools.partial(start, 0))

    acc_ref[...] = jnp.zeros_like(acc_ref)
    xbf_ref[...] = x_ref[...].astype(BF16)

    for f in range(n_f):
        slot = f % FFN_WEIGHT_SLOTS
        if f + 1 < n_f:
            start(f + 1)
        else:
            pl.when(i + 1 < n_tiles)(functools.partial(start, 0))
        for cp in copies(f):
            cp.wait()
        xb = xbf_ref[...]
        gate = _dot(xb, wg_buf[slot])
        up = _dot(xb, wu_buf[slot])
        hid = (gate * _sigmoid(gate)) * up
        acc_ref[...] += _dot(hid.astype(BF16), wo_buf[slot])

    y = alpha * x_ref[...] + 0.5 * acc_ref[...]
    h = _layer_norm_rows(y, g_ref[...], b_ref[...])
    out_ref[...] = h
    for outbf_ref in maybe_outbf:
        outbf_ref[...] = h.astype(BF16)


def ffn_ln(x, w_in_bf, w_out_bf, g, b, *, alpha, emit_bf16, tm=512, tf=512):
    m, d = x.shape
    ff = w_out_bf.shape[0]
    assert m % tm == 0 and ff % tf == 0
    n_f = ff // tf
    n_tiles = m // tm
    assert (n_f - 1) % FFN_WEIGHT_SLOTS != 0
    kern = functools.partial(_ffn_ln_kernel, alpha=alpha, n_f=n_f, n_tiles=n_tiles, tf=tf)
    out_dtypes = [F32, BF16] if emit_bf16 else [F32]
    return pl.pallas_call(
        kern,
        grid=(n_tiles,),
        in_specs=[
            pl.BlockSpec((tm, d), lambda i: (i, 0)),
            pl.BlockSpec(memory_space=pl.ANY),
            pl.BlockSpec(memory_space=pl.ANY),
            pl.BlockSpec((1, d), lambda i: (0, 0)),
            pl.BlockSpec((1, d), lambda i: (0, 0)),
        ],
        out_specs=[pl.BlockSpec((tm, d), lambda i: (i, 0)) for _ in out_dtypes],
        out_shape=[jax.ShapeDtypeStruct((m, d), dt) for dt in out_dtypes],
        scratch_shapes=[
            pltpu.VMEM((tm, d), F32), pltpu.VMEM((tm, d), BF16),
            pltpu.VMEM((FFN_WEIGHT_SLOTS, d, tf), BF16), pltpu.VMEM((FFN_WEIGHT_SLOTS, d, tf), BF16),
            pltpu.VMEM((FFN_WEIGHT_SLOTS, tf, d), BF16),
            pltpu.SemaphoreType.DMA((3, FFN_WEIGHT_SLOTS)),
        ],
        compiler_params=_params(("arbitrary",)),
        name="ffn_ln",
    )(x, w_in_bf, w_out_bf, g.reshape(1, d), b.reshape(1, d))


def _matmul_kernel(x_ref, w_ref, o_ref):
    o_ref[...] = _dot(x_ref[...], w_ref[...]).astype(o_ref.dtype)


def matmul(x_bf, w_bf, out_dtype, *, tm=2048, tn=512):
    m, k = x_bf.shape
    n = w_bf.shape[1]
    assert m % tm == 0 and n % tn == 0
    return pl.pallas_call(
        _matmul_kernel,
        grid=(m // tm, n // tn),
        in_specs=[pl.BlockSpec((tm, k), lambda i, j: (i, 0)),
                  pl.BlockSpec((k, tn), lambda i, j: (0, j))],
        out_specs=pl.BlockSpec((tm, tn), lambda i, j: (i, j)),
        out_shape=jax.ShapeDtypeStruct((m, n), out_dtype),
        compiler_params=_params(("parallel", "arbitrary")),
        name="in_proj",
    )(x_bf, w_bf)


def _hgrn_kernel(hq_ref, hff_ref, hfb_ref, hip_ref, lbf_ref, lbb_ref, hio_ref, hog_ref, ng_ref, out_ref,
                 *scratch, seq, layer):
    c = HGRN_CHUNK
    slab = HGRN_SLAB
    n_slab = seq // slab
    dh = HGRN_HEAD_DIM
    assert slab == 4 * c
    t = pl.program_id(0)
    n_set = (len(scratch) - 2) // 2
    sets = (scratch[:n_set], scratch[n_set:2 * n_set])
    dec_ref, kv_ref = scratch[2 * n_set:]

    @pl.when(t == 0)
    def _():
        for ref in sets[1]:
            ref[...] = jnp.zeros(ref.shape, ref.dtype)

    def lower_bound(lb_ref):
        v = lb_ref[...].astype(F32)
        e = jnp.exp(v - jnp.max(v, axis=0, keepdims=True))
        sm = e / jnp.sum(e, axis=0, keepdims=True)
        return jnp.sum(sm[: layer + 1], axis=0, keepdims=True)

    lbs = (lower_bound(lbf_ref), lower_bound(lbb_ref))
    hf_refs = (hff_ref, hfb_ref)

    ri = lax.broadcasted_iota(jnp.int32, (slab, slab), 0)
    ci = lax.broadcasted_iota(jnp.int32, (slab, slab), 1)
    same_chunk = (ri // c) == (ci // c)
    same_pair = (ri // (2 * c)) == (ci // (2 * c))
    lower = ci <= ri
    upper = ci >= ri
    row_in_chunk = lax.broadcasted_iota(jnp.int32, (slab, dh), 0) % c

    def prep_slab(s, wset):
        qd_ref, qmix_ref, q64_ref, q128_ref, kdt_ref, kmixt_ref, k64t_ref, _ = wset
        r0 = pl.multiple_of(s * slab, slab)
        rows = pl.ds(r0, slab)
        q = hq_ref[rows, :].astype(F32)
        q = q * _sigmoid(q)
        qd, ke, e = [], [], []
        for d in range(2):
            f = lbs[d] + (1.0 - lbs[d]) * _sigmoid(hf_refs[d][rows, :].astype(F32))
            ecum = f
            for sh in (1, 2, 4, 8, 16):
                if d == 0:
                    ecum = ecum * jnp.where(row_in_chunk >= sh, pltpu.roll(ecum, sh, 0), 1.0)
                else:
                    ecum = ecum * jnp.where(row_in_chunk < c - sh, pltpu.roll(ecum, slab - sh, 0), 1.0)
            tot_row = c - 1 if d == 0 else 0
            e.append([ecum[j * c + tot_row: j * c + tot_row + 1, :] for j in range(4)])
            qd_d = q * ecum
            kd_d = (1.0 - f) / ecum
            qd_ref[d, rows, :] = qd_d.astype(BF16)
            kdt_ref[d, s] = kd_d.T.astype(BF16)
            qd.append([qd_d[j * c:(j + 1) * c] for j in range(4)])
            ke.append([kd_d[j * c:(j + 1) * c] * e[d][j] for j in range(4)])

        cat = lambda pieces: jnp.concatenate(pieces, axis=0)
        qmix_ref[rows, :] = cat([qd[1][0], qd[0][1], qd[1][2], qd[0][3]]).astype(BF16)
        kmixt_ref[s] = cat([ke[0][0], ke[1][1], ke[0][2], ke[1][3]]).T.astype(BF16)
        q64_ref[rows, :] = cat([qd[1][0] * e[1][1], qd[1][1], qd[0][2], qd[0][3] * e[0][2]]).astype(BF16)
        k64t_ref[s] = cat([ke[0][0] * e[0][1], ke[0][1], ke[1][2], ke[1][3] * e[1][2]]).T.astype(BF16)
        k128 = []
        for d in range(2):
            e0, e1, e2, e3 = e[d]
            from_start = (None, e0, e0 * e1, e0 * e1 * e2)
            to_end = (e1 * e2 * e3, e2 * e3, e3, None)
            qmul, kmul = (from_start, to_end) if d == 0 else (to_end, from_start)
            mul = lambda x, m: x if m is None else x * m
            q128_ref[rows, d * dh:(d + 1) * dh] = cat([mul(qd[d][j], qmul[j]) for j in range(4)]).astype(BF16)
            k128.append(cat([mul(ke[d][j], kmul[j]) for j in range(4)]).astype(BF16))
            dec_ref[d, pl.ds(pl.multiple_of(s * 8, 8), 8), :] = jnp.broadcast_to(e0 * e1 * e2 * e3, (8, dh))
        kv_ref[s] = _dot_tn(hip_ref[rows, :], jnp.concatenate(k128, axis=1))

    def state_step(i, st, d, snap_ref):
        s = i if d == 0 else n_slab - 1 - i
        snap_ref[s, :, d * dh:(d + 1) * dh] = st
        return st * dec_ref[d, pl.ds(s * 8, 1), :] + kv_ref[s, :, d * dh:(d + 1) * dh]

    def out_slab(s, rset):
        qd_ref, qmix_ref, q64_ref, q128_ref, kdt_ref, kmixt_ref, k64t_ref, snap_ref = rset
        rows = pl.ds(pl.multiple_of(s * slab, slab), slab)
        m_df = _dot(qd_ref[0, rows, :], kdt_ref[0, s])
        m_db = _dot(qd_ref[1, rows, :], kdt_ref[1, s])
        m_32 = _dot(qmix_ref[rows, :], kmixt_ref[s])
        m_64 = _dot(q64_ref[rows, :], k64t_ref[s])
        a = jnp.where(same_chunk,
                      jnp.where(lower, m_df, 0.0) + jnp.where(upper, m_db, 0.0),
                      jnp.where(same_pair, m_32, m_64))
        states = jnp.concatenate([snap_ref[s, :, :dh].T, snap_ref[s, :, dh:].T], axis=0)
        o = _dot(a.astype(BF16), hio_ref[rows, :]) + _dot(q128_ref[rows, :], states.astype(BF16))
        o = o * lax.rsqrt(jnp.mean(o * o, axis=-1, keepdims=True) + LN_EPS)
        og = hog_ref[rows, :].astype(F32)
        out_ref[rows, :] = (o * ng_ref[...] * (og * _sigmoid(og))).astype(out_ref.dtype)

    def run(wset, rset):
        def both(s, carry):
            prep_slab(s, wset)
            out_slab(s, rset)
            return carry

        lax.fori_loop(0, n_slab, both, 0, unroll=HGRN_UNROLL)
        for d in range(2):
            lax.fori_loop(0, n_slab, functools.partial(state_step, d=d, snap_ref=wset[-1]),
                          jnp.zeros((dh, dh), F32))

    for parity in range(2):
        pl.when(lax.rem(t, 2) == parity)(functools.partial(run, sets[parity], sets[1 - parity]))


def hgrn(rest, hf, lb_fwd, lb_bwd, norm_g, *, layer, n_heads, col_hq, col_hi, col_hog):
    bsz, seq, _ = rest.shape
    dh = HGRN_HEAD_DIM
    width = n_heads * dh
    nl = lb_fwd.shape[0]
    slab = HGRN_SLAB
    assert seq % slab == 0
    n_slab = seq // slab
    n_items = bsz * n_heads
    kern = functools.partial(_hgrn_kernel, seq=seq, layer=layer)

    def prep_item(t):
        return jnp.minimum(t, n_items - 1)

    def out_item(t):
        return jnp.maximum(t - 1, 0)

    def seq_cols(item_of, base):
        return lambda t: (item_of(t) // n_heads, 0, base // dh + item_of(t) % n_heads)

    def head_cols(item_of):
        return lambda t: (0, item_of(t) % n_heads)

    return pl.pallas_call(
        kern,
        grid=(n_items + 1,),
        in_specs=[
            pl.BlockSpec((None, seq, dh), seq_cols(prep_item, col_hq)),
            pl.BlockSpec((None, seq, dh), seq_cols(prep_item, 0)),
            pl.BlockSpec((None, seq, dh), seq_cols(prep_item, width)),
            pl.BlockSpec((None, seq, dh), seq_cols(prep_item, col_hi)),
            pl.BlockSpec((nl, dh), head_cols(prep_item)),
            pl.BlockSpec((nl, dh), head_cols(prep_item)),
            pl.BlockSpec((None, seq, dh), seq_cols(out_item, col_hi)),
            pl.BlockSpec((None, seq, dh), seq_cols(out_item, col_hog)),
            pl.BlockSpec((1, dh), head_cols(out_item)),
        ],
        out_specs=pl.BlockSpec((None, seq, dh), seq_cols(out_item, 0)),
        out_shape=jax.ShapeDtypeStruct((bsz, seq, width), BF16),
        scratch_shapes=2 * [
            pltpu.VMEM((2, seq, dh), BF16),
            pltpu.VMEM((seq, dh), BF16),
            pltpu.VMEM((seq, dh), BF16),
            pltpu.VMEM((seq, 2 * dh), BF16),
            pltpu.VMEM((2, n_slab, dh, slab), BF16),
            pltpu.VMEM((n_slab, dh, slab), BF16),
            pltpu.VMEM((n_slab, dh, slab), BF16),
            pltpu.VMEM((n_slab, dh, 2 * dh), F32),
        ] + [
            pltpu.VMEM((2, n_slab * 8, dh), F32),
            pltpu.VMEM((n_slab, dh, 2 * dh), F32),
        ],
        compiler_params=_params(("arbitrary",)),
        name="hgrn",
    )(rest, hf, hf, rest, lb_fwd, lb_bwd, rest, rest, norm_g.reshape(1, width))


ATT_TQ = 128
ATT_HALF = 64
ATT_TK = ATT_TQ + 2 * ATT_HALF


def _attn_kernel(*refs, seq):
    n_g = len(ATTN_GROUPS)
    qkv_refs = refs[: 3 * n_g]
    cos_ref, sin_ref, out_ref = refs[3 * n_g: 3 * n_g + 3]
    qr_ref, kr_ref, vf_ref, qcm_ref, kcm_ref, vcm_ref, og_ref, lse_ref = refs[3 * n_g + 3:]
    dh = HEAD_DIM
    scale = HEAD_DIM ** -0.5

    lane = lax.broadcasted_iota(jnp.int32, (ATT_TK, dh), 1)

    def rope(x_ref, rows):
        t = x_ref[rows, :].astype(F32)
        swapped = jnp.where(lane < ROPE_DIM // 2,
                            pltpu.roll(t, dh - ROPE_DIM // 2, 1),
                            pltpu.roll(t, ROPE_DIM // 2, 1))
        return t * cos_ref[rows, :] + swapped * sin_ref[rows, :]

    qi_rel = lax.broadcasted_iota(jnp.int32, (ATT_TQ, ATT_TK), 0)
    kj_rel = lax.broadcasted_iota(jnp.int32, (ATT_TQ, ATT_TK), 1) - ATT_HALF
    band = jnp.abs(qi_rel - kj_rel) <= ATT_HALF

    for g, (window, dil) in enumerate(ATTN_GROUPS):
        assert window // (2 * dil) == ATT_HALF
        seg = seq // dil
        n_blk = seg // ATT_TQ
        assert n_blk & (n_blk - 1) == 0
        pitch = seg + ATT_TK
        q_ref, k_ref, v_ref = qkv_refs[3 * g: 3 * g + 3]

        def stage(s, carry, dil=dil, q_ref=q_ref, k_ref=k_ref, v_ref=v_ref):
            rows = pl.ds(pl.multiple_of(s * ATT_TK, ATT_TK), ATT_TK)
            q = rope(q_ref, rows)
            k = rope(k_ref, rows)
            if dil == 1:
                pad_rows = pl.ds(pl.multiple_of(s * ATT_TK + ATT_HALF, ATT_HALF), ATT_TK)
                qcm_ref[rows, :] = q.astype(BF16)
                kcm_ref[pad_rows, :] = k.astype(BF16)
                vcm_ref[pad_rows, :] = v_ref[rows, :]
            else:
                qr_ref[rows, :] = q
                kr_ref[rows, :] = k
                vf_ref[rows, :] = v_ref[rows, :].astype(F32)
            return carry

        lax.fori_loop(0, seq // ATT_TK, stage, 0, unroll=4)
        for cls in range(dil):
            base = cls * pitch
            for pad_ref, src_ref in ((kcm_ref, kr_ref), (vcm_ref, vf_ref)):
                pad_ref[pl.ds(base, ATT_HALF), :] = jnp.zeros((ATT_HALF, dh), BF16)
                pad_ref[pl.ds(base + ATT_HALF + seg, ATT_TK - ATT_HALF), :] = (
                    jnp.zeros((ATT_TK - ATT_HALF, dh), BF16))
                if dil > 1:
                    pad_ref[pl.ds(base + ATT_HALF, seg), :] = (
                        src_ref[pl.ds(cls, seg, stride=dil), :].astype(BF16))
            if dil > 1:
                qcm_ref[pl.ds(cls * seg, seg), :] = qr_ref[pl.ds(cls, seg, stride=dil), :].astype(BF16)

        def block(u, carry, g=g, dil=dil, seg=seg, n_blk=n_blk, pitch=pitch):
            cls = lax.shift_right_logical(u, n_blk.bit_length() - 1)
            q0 = (u & (n_blk - 1)) * ATT_TQ
            if dil == 1:
                tok_rows = pl.ds(pl.multiple_of(q0, ATT_TQ), ATT_TQ)
            else:
                tok_rows = pl.ds(cls + dil * q0, ATT_TQ, stride=dil)
            qb = qcm_ref[pl.ds(pl.multiple_of(u * ATT_TQ, ATT_TQ), ATT_TQ), :]
            win = pl.ds(pl.multiple_of(cls * pitch + q0, ATT_TQ), ATT_TK)
            s = _dot_nt(qb, kcm_ref[win, :]) * scale
            kj = kj_rel + q0
            valid = band & (kj >= 0) & (kj < seg)
            s = jnp.where(valid, s, NEG_INF)
            m = jnp.max(s, axis=-1, keepdims=True)
            p = jnp.exp(s - m)
            denom = jnp.sum(p, axis=-1, keepdims=True)
            o = _dot(p.astype(BF16), vcm_ref[win, :]) / denom
            lse = m + jnp.log(denom)
            og_ref[g, tok_rows, :] = o
            lse_ref[g, tok_rows, :] = jnp.broadcast_to(lse, (ATT_TQ, dh))
            return carry

        lax.fori_loop(0, dil * n_blk, block, 0, unroll=16)

    def combine(s, carry):
        rows = pl.ds(pl.multiple_of(s * ATT_TK, ATT_TK), ATT_TK)
        lses = [lse_ref[g, rows, :] for g in range(n_g)]
        mx = functools.reduce(jnp.maximum, lses)
        ws = [jnp.exp(l - mx) for l in lses]
        wsum = functools.reduce(lambda a, b: a + b, ws)
        acc = ws[0] * og_ref[0, rows, :]
        for g in range(1, n_g):
            acc = acc + ws[g] * og_ref[g, rows, :]
        out_ref[rows, :] = (acc / wsum).astype(out_ref.dtype)
        return carry

    lax.fori_loop(0, seq // ATT_TK, combine, 0, unroll=2)


def _rope_tables(seq):
    inv_freq = ROPE_THETA ** (-jnp.arange(0, ROPE_DIM, 2, dtype=F32) / ROPE_DIM)
    ang = jnp.arange(seq).astype(F32)[:, None] * inv_freq
    cos, sin = jnp.cos(ang), jnp.sin(ang)
    pad1 = jnp.ones((seq, HEAD_DIM - ROPE_DIM), F32)
    pad0 = jnp.zeros((seq, HEAD_DIM - ROPE_DIM), F32)
    return (jnp.concatenate([cos, cos, pad1], axis=1),
            jnp.concatenate([-sin, sin, pad0], axis=1))


def attn(rest, *, col_qkv):
    bsz, seq, _ = rest.shape
    dh = HEAD_DIM
    hpg = ATTN_HEADS_PER_GROUP
    n_g = len(ATTN_GROUPS)
    assert seq % (max(d for _, d in ATTN_GROUPS) * ATT_TQ) == 0
    cosf, sinf = _rope_tables(seq)

    def col(g, t):
        base = col_qkv // dh + (g * 3 + t) * hpg
        return lambda b, j: (b, 0, base + j)

    in_specs = [pl.BlockSpec((None, seq, dh), col(g, t)) for g in range(n_g) for t in range(3)]
    in_specs += [pl.BlockSpec((seq, dh), lambda b, j: (0, 0)) for _ in range(2)]
    pad_rows = max(dil * (seq // dil + ATT_TK) for _, dil in ATTN_GROUPS)
    return pl.pallas_call(
        functools.partial(_attn_kernel, seq=seq),
        grid=(bsz, hpg),
        in_specs=in_specs,
        out_specs=pl.BlockSpec((None, seq, dh), lambda b, j: (b, 0, j)),
        out_shape=jax.ShapeDtypeStruct((bsz, seq, hpg * dh), BF16),
        scratch_shapes=[
            pltpu.VMEM((seq, dh), F32), pltpu.VMEM((seq, dh), F32), pltpu.VMEM((seq, dh), F32),
            pltpu.VMEM((seq, dh), BF16),
            pltpu.VMEM((pad_rows, dh), BF16), pltpu.VMEM((pad_rows, dh), BF16),
            pltpu.VMEM((n_g, seq, dh), F32), pltpu.VMEM((n_g, seq, dh), F32),
        ],
        compiler_params=_params(("parallel", "parallel")),
        name="attn",
    )(*([rest] * (3 * n_g)), cosf, sinf)


def _mix_out_kernel(oa_ref, ob_ref, ga_ref, gb_ref, h_ref, wa_ref, wb_ref, wo_ref, g_ref, b_ref,
                    out_ref, *, alpha):
    ya = _dot(oa_ref[...], wa_ref[...])
    yb = _dot(ob_ref[...], wb_ref[...])
    z = _sigmoid(ga_ref[...].astype(F32)) * ya + _sigmoid(gb_ref[...].astype(F32)) * yb
    mix = _dot(z.astype(BF16), wo_ref[...])
    out_ref[...] = _layer_norm_rows(alpha * h_ref[...] + mix, g_ref[...], b_ref[...])


def mix_out(oa, ob, rest2d, h, wa_bf, wb_bf, wo_bf, g, b, *, alpha, col_gate, tm=256):
    m, d = h.shape
    wa_w = oa.shape[1]
    wb_w = ob.shape[1]
    assert m % tm == 0 and col_gate % d == 0
    const = lambda i: (0, 0)
    single = pl.Buffered(1)
    return pl.pallas_call(
        functools.partial(_mix_out_kernel, alpha=alpha),
        grid=(m // tm,),
        in_specs=[
            pl.BlockSpec((tm, wa_w), lambda i: (i, 0)),
            pl.BlockSpec((tm, wb_w), lambda i: (i, 0)),
            pl.BlockSpec((tm, d), lambda i: (i, col_gate // d)),
            pl.BlockSpec((tm, d), lambda i: (i, col_gate // d + 1)),
            pl.BlockSpec((tm, d), lambda i: (i, 0)),
            pl.BlockSpec((wa_w, d), const, pipeline_mode=single),
            pl.BlockSpec((wb_w, d), const, pipeline_mode=single),
            pl.BlockSpec((d, d), const, pipeline_mode=single),
            pl.BlockSpec((1, d), const),
            pl.BlockSpec((1, d), const),
        ],
        out_specs=pl.BlockSpec((tm, d), lambda i: (i, 0)),
        out_shape=jax.ShapeDtypeStruct((m, d), F32),
        compiler_params=_params(("parallel",)),
        name="mix_out",
    )(oa, ob, rest2d, rest2d, h, wa_bf, wb_bf, wo_bf, g.reshape(1, d), b.reshape(1, d))


def kernel(x, ffn1_w_in, ffn1_w_out, ln1_g, ln1_b, mix_w_in, hgrn_lb_fwd, hgrn_lb_bwd, hgrn_norm_g,
           w_branch_a, w_branch_b, mix_w_out, ln2_g, ln2_b, ffn2_w_in, ffn2_w_out, ln3_g, ln3_b):
    bsz, seq, d = x.shape
    depth = ffn1_w_in.shape[0]
    alpha = (2.0 * depth) ** 0.25
    hw = w_branch_a.shape[1]
    n_heads = hw // HGRN_HEAD_DIM
    qkv_w = len(ATTN_GROUPS) * 3 * ATTN_HEADS_PER_GROUP * HEAD_DIM
    m = bsz * seq

    h = x.reshape(m, d)
    for layer in range(depth):
        h, h_bf = ffn_ln(h, ffn1_w_in[layer].astype(BF16), ffn1_w_out[layer].astype(BF16),
                         ln1_g[layer], ln1_b[layer], alpha=alpha, emit_bf16=True)

        w_in = mix_w_in[layer]
        w_hf = w_in[:, hw:3 * hw].astype(BF16)
        w_rest = jnp.concatenate([w_in[:, :hw], w_in[:, 3 * hw:5 * hw], w_in[:, 5 * hw + qkv_w:],
                                  w_in[:, 5 * hw:5 * hw + qkv_w]], axis=1).astype(BF16)
        hf = matmul(h_bf, w_hf, F32).reshape(bsz, seq, 2 * hw)
        rest2d = matmul(h_bf, w_rest, BF16, tm=1024, tn=w_rest.shape[1] // 4)
        rest = rest2d.reshape(bsz, seq, -1)
        col_hq, col_hi, col_hog, col_gate = 0, hw, 2 * hw, 3 * hw
        col_qkv = 3 * hw + 2 * d

        o_a = hgrn(rest, hf, hgrn_lb_fwd, hgrn_lb_bwd, hgrn_norm_g[layer], layer=layer,
                   n_heads=n_heads, col_hq=col_hq, col_hi=col_hi, col_hog=col_hog)
        o_b = attn(rest, col_qkv=col_qkv)

        h = mix_out(o_a.reshape(m, hw), o_b.reshape(m, -1), rest2d, h,
                    w_branch_a[layer].astype(BF16), w_branch_b[layer].astype(BF16),
                    mix_w_out[layer].astype(BF16), ln2_g[layer], ln2_b[layer],
                    alpha=alpha, col_gate=col_gate)

        (h,) = ffn_ln(h, ffn2_w_in[layer].astype(BF16), ffn2_w_out[layer].astype(BF16),
                      ln3_g[layer], ln3_b[layer], alpha=alpha, emit_bf16=False)
    return h.reshape(bsz, seq, d)
```

```python
import functools

import jax
import jax.numpy as jnp
from jax import lax
from jax.experimental import pallas as pl
from jax.experimental.pallas import tpu as pltpu

F32 = jnp.float32
BF16 = jnp.bfloat16

HGRN_HEAD_DIM = 128
HGRN_CHUNK = 32
HGRN_SLAB = 4 * HGRN_CHUNK
HGRN_UNROLL = 16
ATTN_GROUPS = ((128, 1), (512, 4), (2048, 16))
ATTN_HEADS_PER_GROUP = 4
HEAD_DIM = 128
ROPE_THETA = 500000.0
ROPE_DIM = HEAD_DIM // 4
LN_EPS = 1e-5
NEG_INF = -1e30

V7X_VMEM_LIMIT_BYTES = 60000 * 1024


def _params(semantics):
    return pltpu.CompilerParams(dimension_semantics=semantics,
                                vmem_limit_bytes=V7X_VMEM_LIMIT_BYTES)


def _dot(a, b):
    return jnp.dot(a, b, preferred_element_type=F32)


def _dot_nt(a, b):
    return lax.dot_general(a, b, (((1,), (1,)), ((), ())), preferred_element_type=F32)


def _dot_tn(a, b):
    return lax.dot_general(a, b, (((0,), (0,)), ((), ())), preferred_element_type=F32)


def _sigmoid(x):
    return 1.0 / (1.0 + jnp.exp(-x))


def _layer_norm_rows(y, g, b):
    mu = jnp.mean(y, axis=-1, keepdims=True)
    d = y - mu
    var = jnp.mean(d * d, axis=-1, keepdims=True)
    return d * lax.rsqrt(var + LN_EPS) * g + b


FFN_WEIGHT_SLOTS = 3
FFN_HID_COLS = 256


def _ffn_ln_kernel(x_ref, win_hbm, wout_hbm, g_ref, b_ref, *rest, alpha, n_f, n_tiles, tf):
    (out_ref, *maybe_outbf), (acc_ref, xbf_ref, wg_buf, wu_buf, wo_buf, sems) = rest[:-6], rest[-6:]
    i = pl.program_id(0)
    ff = n_f * tf

    def copies(f):
        slot = f % FFN_WEIGHT_SLOTS
        return (
            pltpu.make_async_copy(win_hbm.at[:, pl.ds(f * tf, tf)], wg_buf.at[slot], sems.at[0, slot]),
            pltpu.make_async_copy(win_hbm.at[:, pl.ds(ff + f * tf, tf)], wu_buf.at[slot], sems.at[1, slot]),
            pltpu.make_async_copy(wout_hbm.at[pl.ds(f * tf, tf), :], wo_buf.at[slot], sems.at[2, slot]),
        )

    def start(f):
        for cp in copies(f):
            cp.start()

    pl.when(i == 0)(functools.partial(start, 0))

    acc_ref[...] = jnp.zeros_like(acc_ref)
    xbf_ref[...] = x_ref[...].astype(BF16)

    for f in range(n_f):
        slot = f % FFN_WEIGHT_SLOTS
        if f + 1 < n_f:
            start(f + 1)
        else:
            pl.when(i + 1 < n_tiles)(functools.partial(start, 0))
        for cp in copies(f):
            cp.wait()
        xb = xbf_ref[...]
        part = None
        for h0 in range(0, tf, FFN_HID_COLS):
            cols = pl.ds(h0, FFN_HID_COLS)
            gate = _dot(xb, wg_buf[slot, :, cols])
            up = _dot(xb, wu_buf[slot, :, cols])
            hid = (gate * _sigmoid(gate)) * up
            piece = _dot(hid.astype(BF16), wo_buf[slot, cols, :])
            part = piece if part is None else part + piece
        acc_ref[...] += part

    y = alpha * x_ref[...] + 0.5 * acc_ref[...]
    h = _layer_norm_rows(y, g_ref[...], b_ref[...])
    out_ref[...] = h
    for outbf_ref in maybe_outbf:
        outbf_ref[...] = h.astype(BF16)


def ffn_ln(x, w_in_bf, w_out_bf, g, b, *, alpha, emit_bf16, tm=512, tf=512):
    m, d = x.shape
    ff = w_out_bf.shape[0]
    assert m % tm == 0 and ff % tf == 0
    n_f = ff // tf
    n_tiles = m // tm
    assert (n_f - 1) % FFN_WEIGHT_SLOTS != 0
    kern = functools.partial(_ffn_ln_kernel, alpha=alpha, n_f=n_f, n_tiles=n_tiles, tf=tf)
    out_dtypes = [F32, BF16] if emit_bf16 else [F32]
    return pl.pallas_call(
        kern,
        grid=(n_tiles,),
        in_specs=[
            pl.BlockSpec((tm, d), lambda i: (i, 0)),
            pl.BlockSpec(memory_space=pl.ANY),
            pl.BlockSpec(memory_space=pl.ANY),
            pl.BlockSpec((1, d), lambda i: (0, 0)),
            pl.BlockSpec((1, d), lambda i: (0, 0)),
        ],
        out_specs=[pl.BlockSpec((tm, d), lambda i: (i, 0)) for _ in out_dtypes],
        out_shape=[jax.ShapeDtypeStruct((m, d), dt) for dt in out_dtypes],
        scratch_shapes=[
            pltpu.VMEM((tm, d), F32), pltpu.VMEM((tm, d), BF16),
            pltpu.VMEM((FFN_WEIGHT_SLOTS, d, tf), BF16), pltpu.VMEM((FFN_WEIGHT_SLOTS, d, tf), BF16),
            pltpu.VMEM((FFN_WEIGHT_SLOTS, tf, d), BF16),
            pltpu.SemaphoreType.DMA((3, FFN_WEIGHT_SLOTS)),
        ],
        compiler_params=_params(("arbitrary",)),
        name="ffn_ln",
    )(x, w_in_bf, w_out_bf, g.reshape(1, d), b.reshape(1, d))


def _matmul_kernel(x_ref, w_ref, o_ref):
    o_ref[...] = _dot(x_ref[...], w_ref[...]).astype(o_ref.dtype)


def matmul(x_bf, w_bf, out_dtype, *, tm=2048, tn=512):
    m, k = x_bf.shape
    n = w_bf.shape[1]
    assert m % tm == 0 and n % tn == 0
    return pl.pallas_call(
        _matmul_kernel,
        grid=(m // tm, n // tn),
        in_specs=[pl.BlockSpec((tm, k), lambda i, j: (i, 0)),
                  pl.BlockSpec((k, tn), lambda i, j: (0, j))],
        out_specs=pl.BlockSpec((tm, tn), lambda i, j: (i, j)),
        out_shape=jax.ShapeDtypeStruct((m, n), out_dtype),
        compiler_params=_params(("parallel", "arbitrary")),
        name="in_proj",
    )(x_bf, w_bf)


def _hgrn_kernel(hq_ref, hff_ref, hfb_ref, hip_ref, lbf_ref, lbb_ref, hio_ref, hog_ref, ng_ref, out_ref,
                 *scratch, seq, layer):
    c = HGRN_CHUNK
    slab = HGRN_SLAB
    n_slab = seq // slab
    dh = HGRN_HEAD_DIM
    assert slab == 4 * c
    t = pl.program_id(0)
    n_set = (len(scratch) - 2) // 2
    sets = (scratch[:n_set], scratch[n_set:2 * n_set])
    dec_ref, kv_ref = scratch[2 * n_set:]

    @pl.when(t == 0)
    def _():
        for ref in sets[1]:
            ref[...] = jnp.zeros(ref.shape, ref.dtype)

    def lower_bound(lb_ref):
        v = lb_ref[...].astype(F32)
        e = jnp.exp(v - jnp.max(v, axis=0, keepdims=True))
        sm = e / jnp.sum(e, axis=0, keepdims=True)
        return jnp.sum(sm[: layer + 1], axis=0, keepdims=True)

    lbs = (lower_bound(lbf_ref), lower_bound(lbb_ref))
    hf_refs = (hff_ref, hfb_ref)

    ri = lax.broadcasted_iota(jnp.int32, (slab, slab), 0)
    ci = lax.broadcasted_iota(jnp.int32, (slab, slab), 1)
    same_chunk = (ri // c) == (ci // c)
    same_pair = (ri // (2 * c)) == (ci // (2 * c))
    lower = ci <= ri
    upper = ci >= ri
    row_in_chunk = lax.broadcasted_iota(jnp.int32, (slab, dh), 0) % c

    def prep_slab(s, wset):
        qd_ref, qmix_ref, q64_ref, q128_ref, kdt_ref, kmixt_ref, k64t_ref, _ = wset
        r0 = pl.multiple_of(s * slab, slab)
        rows = pl.ds(r0, slab)
        q = hq_ref[rows, :].astype(F32)
        q = q * _sigmoid(q)
        qd, ke, e = [], [], []
        for d in range(2):
            f = lbs[d] + (1.0 - lbs[d]) * _sigmoid(hf_refs[d][rows, :].astype(F32))
            ecum = f
            for sh in (1, 2, 4, 8, 16):
                if d == 0:
                    ecum = ecum * jnp.where(row_in_chunk >= sh, pltpu.roll(ecum, sh, 0), 1.0)
                else:
                    ecum = ecum * jnp.where(row_in_chunk < c - sh, pltpu.roll(ecum, slab - sh, 0), 1.0)
            tot_row = c - 1 if d == 0 else 0
            e.append([ecum[j * c + tot_row: j * c + tot_row + 1, :] for j in range(4)])
            qd_d = q * ecum
            kd_d = (1.0 - f) / ecum
            qd_ref[d, rows, :] = qd_d.astype(BF16)
            kdt_ref[d, s] = kd_d.T.astype(BF16)
            qd.append([qd_d[j * c:(j + 1) * c] for j in range(4)])
            ke.append([kd_d[j * c:(j + 1) * c] * e[d][j] for j in range(4)])

        cat = lambda pieces: jnp.concatenate(pieces, axis=0)
        qmix_ref[rows, :] = cat([qd[1][0], qd[0][1], qd[1][2], qd[0][3]]).astype(BF16)
        kmixt_ref[s] = cat([ke[0][0], ke[1][1], ke[0][2], ke[1][3]]).T.astype(BF16)
        q64_ref[rows, :] = cat([qd[1][0] * e[1][1], qd[1][1], qd[0][2], qd[0][3] * e[0][2]]).astype(BF16)
        k64t_ref[s] = cat([ke[0][0] * e[0][1], ke[0][1], ke[1][2], ke[1][3] * e[1][2]]).T.astype(BF16)
        k128 = []
        for d in range(2):
            e0, e1, e2, e3 = e[d]
            from_start = (None, e0, e0 * e1, e0 * e1 * e2)
            to_end = (e1 * e2 * e3, e2 * e3, e3, None)
            qmul, kmul = (from_start, to_end) if d == 0 else (to_end, from_start)
            mul = lambda x, m: x if m is None else x * m
            q128_ref[rows, d * dh:(d + 1) * dh] = cat([mul(qd[d][j], qmul[j]) for j in range(4)]).astype(BF16)
            k128.append(cat([mul(ke[d][j], kmul[j]) for j in range(4)]).astype(BF16))
            dec_ref[d, pl.ds(pl.multiple_of(s * 8, 8), 8), :] = jnp.broadcast_to(e0 * e1 * e2 * e3, (8, dh))
        kv_ref[s] = _dot_tn(hip_ref[rows, :], jnp.concatenate(k128, axis=1))

    def state_step(i, st, d, snap_ref):
        s = i if d == 0 else n_slab - 1 - i
        snap_ref[s, :, d * dh:(d + 1) * dh] = st
        return st * dec_ref[d, pl.ds(s * 8, 1), :] + kv_ref[s, :, d * dh:(d + 1) * dh]

    def out_slab(s, rset):
        qd_ref, qmix_ref, q64_ref, q128_ref, kdt_ref, kmixt_ref, k64t_ref, snap_ref = rset
        rows = pl.ds(pl.multiple_of(s * slab, slab), slab)
        m_df = _dot(qd_ref[0, rows, :], kdt_ref[0, s])
        m_db = _dot(qd_ref[1, rows, :], kdt_ref[1, s])
        m_32 = _dot(qmix_ref[rows, :], kmixt_ref[s])
        m_64 = _dot(q64_ref[rows, :], k64t_ref[s])
        a = jnp.where(same_chunk,
                      jnp.where(lower, m_df, 0.0) + jnp.where(upper, m_db, 0.0),
                      jnp.where(same_pair, m_32, m_64))
        states = jnp.concatenate([snap_ref[s, :, :dh].T, snap_ref[s, :, dh:].T], axis=0)
        o = _dot(a.astype(BF16), hio_ref[rows, :]) + _dot(q128_ref[rows, :], states.astype(BF16))
        o = o * lax.rsqrt(jnp.mean(o * o, axis=-1, keepdims=True) + LN_EPS)
        og = hog_ref[rows, :].astype(F32)
        out_ref[rows, :] = (o * ng_ref[...] * (og * _sigmoid(og))).astype(out_ref.dtype)

    def run(wset, rset):
        def both(s, carry):
            prep_slab(s, wset)
            out_slab(s, rset)
            return carry

        lax.fori_loop(0, n_slab, both, 0, unroll=HGRN_UNROLL)
        for d in range(2):
            lax.fori_loop(0, n_slab, functools.partial(state_step, d=d, snap_ref=wset[-1]),
                          jnp.zeros((dh, dh), F32))

    for parity in range(2):
        pl.when(lax.rem(t, 2) == parity)(functools.partial(run, sets[parity], sets[1 - parity]))


def hgrn(rest, hf, lb_fwd, lb_bwd, norm_g, *, layer, n_heads, col_hq, col_hi, col_hog):
    bsz, seq, _ = rest.shape
    dh = HGRN_HEAD_DIM
    width = n_heads * dh
    nl = lb_fwd.shape[0]
    slab = HGRN_SLAB
    assert seq % slab == 0
    n_slab = seq // slab
    n_items = bsz * n_heads
    kern = functools.partial(_hgrn_kernel, seq=seq, layer=layer)

    def prep_item(t):
        return jnp.minimum(t, n_items - 1)

    def out_item(t):
        return jnp.maximum(t - 1, 0)

    def seq_cols(item_of, base):
        return lambda t: (item_of(t) // n_heads, 0, base // dh + item_of(t) % n_heads)

    def head_cols(item_of):
        return lambda t: (0, item_of(t) % n_heads)

    return pl.pallas_call(
        kern,
        grid=(n_items + 1,),
        in_specs=[
            pl.BlockSpec((None, seq, dh), seq_cols(prep_item, col_hq)),
            pl.BlockSpec((None, seq, dh), seq_cols(prep_item, 0)),
            pl.BlockSpec((None, seq, dh), seq_cols(prep_item, width)),
            pl.BlockSpec((None, seq, dh), seq_cols(prep_item, col_hi)),
            pl.BlockSpec((nl, dh), head_cols(prep_item)),
            pl.BlockSpec((nl, dh), head_cols(prep_item)),
            pl.BlockSpec((None, seq, dh), seq_cols(out_item, col_hi)),
            pl.BlockSpec((None, seq, dh), seq_cols(out_item, col_hog)),
            pl.BlockSpec((1, dh), head_cols(out_item)),
        ],
        out_specs=pl.BlockSpec((None, seq, dh), seq_cols(out_item, 0)),
        out_shape=jax.ShapeDtypeStruct((bsz, seq, width), BF16),
        scratch_shapes=2 * [
            pltpu.VMEM((2, seq, dh), BF16),
            pltpu.VMEM((seq, dh), BF16),
            pltpu.VMEM((seq, dh), BF16),
            pltpu.VMEM((seq, 2 * dh), BF16),
            pltpu.VMEM((2, n_slab, dh, slab), BF16),
            pltpu.VMEM((n_slab, dh, slab), BF16),
            pltpu.VMEM((n_slab, dh, slab), BF16),
            pltpu.VMEM((n_slab, dh, 2 * dh), F32),
        ] + [
            pltpu.VMEM((2, n_slab * 8, dh), F32),
            pltpu.VMEM((n_slab, dh, 2 * dh), F32),
        ],
        compiler_params=_params(("arbitrary",)),
        name="hgrn",
    )(rest, hf, hf, rest, lb_fwd, lb_bwd, rest, rest, norm_g.reshape(1, width))


ATT_TQ = 128
ATT_HALF = 64
ATT_TK = ATT_TQ + 2 * ATT_HALF


def _attn_kernel(*refs, seq):
    n_g = len(ATTN_GROUPS)
    qkv_refs = refs[: 3 * n_g]
    cos_ref, sin_ref, out_ref = refs[3 * n_g: 3 * n_g + 3]
    qr_ref, kr_ref, vf_ref, qcm_ref, kcm_ref, vcm_ref, og_ref, lse_ref = refs[3 * n_g + 3:]
    dh = HEAD_DIM
    scale = HEAD_DIM ** -0.5

    lane = lax.broadcasted_iota(jnp.int32, (ATT_TK, dh), 1)

    def rope(x_ref, rows):
        t = x_ref[rows, :].astype(F32)
        swapped = jnp.where(lane < ROPE_DIM // 2,
                            pltpu.roll(t, dh - ROPE_DIM // 2, 1),
                            pltpu.roll(t, ROPE_DIM // 2, 1))
        return t * cos_ref[rows, :] + swapped * sin_ref[rows, :]

    qi_rel = lax.broadcasted_iota(jnp.int32, (ATT_TQ, ATT_TK), 0)
    kj_rel = lax.broadcasted_iota(jnp.int32, (ATT_TQ, ATT_TK), 1) - ATT_HALF
    band = jnp.abs(qi_rel - kj_rel) <= ATT_HALF

    for g, (window, dil) in enumerate(ATTN_GROUPS):
        assert window // (2 * dil) == ATT_HALF
        seg = seq // dil
        n_blk = seg // ATT_TQ
        assert n_blk & (n_blk - 1) == 0
        pitch = seg + ATT_TK
        q_ref, k_ref, v_ref = qkv_refs[3 * g: 3 * g + 3]

        def stage(s, carry, dil=dil, q_ref=q_ref, k_ref=k_ref, v_ref=v_ref):
            rows = pl.ds(pl.multiple_of(s * ATT_TK, ATT_TK), ATT_TK)
            q = rope(q_ref, rows)
            k = rope(k_ref, rows)
            if dil == 1:
                pad_rows = pl.ds(pl.multiple_of(s * ATT_TK + ATT_HALF, ATT_HALF), ATT_TK)
                qcm_ref[rows, :] = q.astype(BF16)
                kcm_ref[pad_rows, :] = k.astype(BF16)
                vcm_ref[pad_rows, :] = v_ref[rows, :]
            else:
                qr_ref[rows, :] = q
                kr_ref[rows, :] = k
                vf_ref[rows, :] = v_ref[rows, :].astype(F32)
            return carry

        lax.fori_loop(0, seq // ATT_TK, stage, 0, unroll=4)
        for cls in range(dil):
            base = cls * pitch
            for pad_ref, src_ref in ((kcm_ref, kr_ref), (vcm_ref, vf_ref)):
                pad_ref[pl.ds(base, ATT_HALF), :] = jnp.zeros((ATT_HALF, dh), BF16)
                pad_ref[pl.ds(base + ATT_HALF + seg, ATT_TK - ATT_HALF), :] = (
                    jnp.zeros((ATT_TK - ATT_HALF, dh), BF16))
                if dil > 1:
                    pad_ref[pl.ds(base + ATT_HALF, seg), :] = (
                        src_ref[pl.ds(cls, seg, stride=dil), :].astype(BF16))
            if dil > 1:
                qcm_ref[pl.ds(cls * seg, seg), :] = qr_ref[pl.ds(cls, seg, stride=dil), :].astype(BF16)

        def block(u, carry, g=g, dil=dil, seg=seg, n_blk=n_blk, pitch=pitch):
            cls = lax.shift_right_logical(u, n_blk.bit_length() - 1)
            q0 = (u & (n_blk - 1)) * ATT_TQ
            if dil == 1:
                tok_rows = pl.ds(pl.multiple_of(q0, ATT_TQ), ATT_TQ)
            else:
                tok_rows = pl.ds(cls + dil * q0, ATT_TQ, stride=dil)
            qb = qcm_ref[pl.ds(pl.multiple_of(u * ATT_TQ, ATT_TQ), ATT_TQ), :]
            win = pl.ds(pl.multiple_of(cls * pitch + q0, ATT_TQ), ATT_TK)
            s = _dot_nt(qb, kcm_ref[win, :]) * scale
            kj = kj_rel + q0
            valid = band & (kj >= 0) & (kj < seg)
            s = jnp.where(valid, s, NEG_INF)
            m = jnp.max(s, axis=-1, keepdims=True)
            p = jnp.exp(s - m)
            denom = jnp.sum(p, axis=-1, keepdims=True)
            o = _dot(p.astype(BF16), vcm_ref[win, :]) / denom
            lse = m + jnp.log(denom)
            og_ref[g, tok_rows, :] = o
            lse_ref[g, tok_rows, :] = jnp.broadcast_to(lse, (ATT_TQ, dh))
            return carry

        lax.fori_loop(0, dil * n_blk, block, 0, unroll=16)

    def combine(s, carry):
        rows = pl.ds(pl.multiple_of(s * ATT_TK, ATT_TK), ATT_TK)
        lses = [lse_ref[g, rows, :] for g in range(n_g)]
        mx = functools.reduce(jnp.maximum, lses)
        ws = [jnp.exp(l - mx) for l in lses]
        wsum = functools.reduce(lambda a, b: a + b, ws)
        acc = ws[0] * og_ref[0, rows, :]
        for g in range(1, n_g):
            acc = acc + ws[g] * og_ref[g, rows, :]
        out_ref[rows, :] = (acc / wsum).astype(out_ref.dtype)
        return carry

    lax.fori_loop(0, seq // ATT_TK, combine, 0, unroll=2)


def _rope_tables(seq):
    inv_freq = ROPE_THETA ** (-jnp.arange(0, ROPE_DIM, 2, dtype=F32) / ROPE_DIM)
    ang = jnp.arange(seq).astype(F32)[:, None] * inv_freq
    cos, sin = jnp.cos(ang), jnp.sin(ang)
    pad1 = jnp.ones((seq, HEAD_DIM - ROPE_DIM), F32)
    pad0 = jnp.zeros((seq, HEAD_DIM - ROPE_DIM), F32)
    return (jnp.concatenate([cos, cos, pad1], axis=1),
            jnp.concatenate([-sin, sin, pad0], axis=1))


def attn(rest, *, col_qkv):
    bsz, seq, _ = rest.shape
    dh = HEAD_DIM
    hpg = ATTN_HEADS_PER_GROUP
    n_g = len(ATTN_GROUPS)
    assert seq % (max(d for _, d in ATTN_GROUPS) * ATT_TQ) == 0
    cosf, sinf = _rope_tables(seq)

    def col(g, t):
        base = col_qkv // dh + (g * 3 + t) * hpg
        return lambda b, j: (b, 0, base + j)

    in_specs = [pl.BlockSpec((None, seq, dh), col(g, t)) for g in range(n_g) for t in range(3)]
    in_specs += [pl.BlockSpec((seq, dh), lambda b, j: (0, 0)) for _ in range(2)]
    pad_rows = max(dil * (seq // dil + ATT_TK) for _, dil in ATTN_GROUPS)
    return pl.pallas_call(
        functools.partial(_attn_kernel, seq=seq),
        grid=(bsz, hpg),
        in_specs=in_specs,
        out_specs=pl.BlockSpec((None, seq, dh), lambda b, j: (b, 0, j)),
        out_shape=jax.ShapeDtypeStruct((bsz, seq, hpg * dh), BF16),
        scratch_shapes=[
            pltpu.VMEM((seq, dh), F32), pltpu.VMEM((seq, dh), F32), pltpu.VMEM((seq, dh), F32),
            pltpu.VMEM((seq, dh), BF16),
            pltpu.VMEM((pad_rows, dh), BF16), pltpu.VMEM((pad_rows, dh), BF16),
            pltpu.VMEM((n_g, seq, dh), F32), pltpu.VMEM((n_g, seq, dh), F32),
        ],
        compiler_params=_params(("parallel", "parallel")),
        name="attn",
    )(*([rest] * (3 * n_g)), cosf, sinf)


def _mix_out_kernel(oa_ref, ob_ref, ga_ref, gb_ref, h_ref, wa_ref, wb_ref, wo_ref, g_ref, b_ref,
                    out_ref, *, alpha):
    ya = _dot(oa_ref[...], wa_ref[...])
    yb = _dot(ob_ref[...], wb_ref[...])
    z = _sigmoid(ga_ref[...].astype(F32)) * ya + _sigmoid(gb_ref[...].astype(F32)) * yb
    mix = _dot(z.astype(BF16), wo_ref[...])
    out_ref[...] = _layer_norm_rows(alpha * h_ref[...] + mix, g_ref[...], b_ref[...])


def mix_out(oa, ob, rest2d, h, wa_bf, wb_bf, wo_bf, g, b, *, alpha, col_gate, tm=256):
    m, d = h.shape
    wa_w = oa.shape[1]
    wb_w = ob.shape[1]
    assert m % tm == 0 and col_gate % d == 0
    const = lambda i: (0, 0)
    single = pl.Buffered(1)
    return pl.pallas_call(
        functools.partial(_mix_out_kernel, alpha=alpha),
        grid=(m // tm,),
        in_specs=[
            pl.BlockSpec((tm, wa_w), lambda i: (i, 0)),
            pl.BlockSpec((tm, wb_w), lambda i: (i, 0)),
            pl.BlockSpec((tm, d), lambda i: (i, col_gate // d)),
            pl.BlockSpec((tm, d), lambda i: (i, col_gate // d + 1)),
            pl.BlockSpec((tm, d), lambda i: (i, 0)),
            pl.BlockSpec((wa_w, d), const, pipeline_mode=single),
            pl.BlockSpec((wb_w, d), const, pipeline_mode=single),
            pl.BlockSpec((d, d), const, pipeline_mode=single),
            pl.BlockSpec((1, d), const),
            pl.BlockSpec((1, d), const),
        ],
        out_specs=pl.BlockSpec((tm, d), lambda i: (i, 0)),
        out_shape=jax.ShapeDtypeStruct((m, d), F32),
        compiler_params=_params(("parallel",)),
        name="mix_out",
    )(oa, ob, rest2d, rest2d, h, wa_bf, wb_bf, wo_bf, g.reshape(1, d), b.reshape(1, d))


def kernel(x, ffn1_w_in, ffn1_w_out, ln1_g, ln1_b, mix_w_in, hgrn_lb_fwd, hgrn_lb_bwd, hgrn_norm_g,
           w_branch_a, w_branch_b, mix_w_out, ln2_g, ln2_b, ffn2_w_in, ffn2_w_out, ln3_g, ln3_b):
    bsz, seq, d = x.shape
    depth = ffn1_w_in.shape[0]
    alpha = (2.0 * depth) ** 0.25
    hw = w_branch_a.shape[1]
    n_heads = hw // HGRN_HEAD_DIM
    qkv_w = len(ATTN_GROUPS) * 3 * ATTN_HEADS_PER_GROUP * HEAD_DIM
    m = bsz * seq

    h = x.reshape(m, d)
    for layer in range(depth):
        h, h_bf = ffn_ln(h, ffn1_w_in[layer].astype(BF16), ffn1_w_out[layer].astype(BF16),
                         ln1_g[layer], ln1_b[layer], alpha=alpha, emit_bf16=True)

        w_in = mix_w_in[layer]
        w_hf = w_in[:, hw:3 * hw].astype(BF16)
        w_rest = jnp.concatenate([w_in[:, :hw], w_in[:, 3 * hw:5 * hw], w_in[:, 5 * hw + qkv_w:],
                                  w_in[:, 5 * hw:5 * hw + qkv_w]], axis=1).astype(BF16)
        hf = matmul(h_bf, w_hf, F32).reshape(bsz, seq, 2 * hw)
        rest2d = matmul(h_bf, w_rest, BF16, tm=1024, tn=w_rest.shape[1] // 4)
        rest = rest2d.reshape(bsz, seq, -1)
        col_hq, col_hi, col_hog, col_gate = 0, hw, 2 * hw, 3 * hw
        col_qkv = 3 * hw + 2 * d

        o_a = hgrn(rest, hf, hgrn_lb_fwd, hgrn_lb_bwd, hgrn_norm_g[layer], layer=layer,
                   n_heads=n_heads, col_hq=col_hq, col_hi=col_hi, col_hog=col_hog)
        o_b = attn(rest, col_qkv=col_qkv)

        h = mix_out(o_a.reshape(m, hw), o_b.reshape(m, -1), rest2d, h,
                    w_branch_a[layer].astype(BF16), w_branch_b[layer].astype(BF16),
                    mix_w_out[layer].astype(BF16), ln2_g[layer], ln2_b[layer],
                    alpha=alpha, col_gate=col_gate)

        (h,) = ffn_ln(h, ffn2_w_in[layer].astype(BF16), ffn2_w_out[layer].astype(BF16),
                      ln3_g[layer], ln3_b[layer], alpha=alpha, emit_bf16=False)
    return h.reshape(bsz, seq, d)
```
